```python
import math
import jax, jax.numpy as jnp
from jax import lax
import numpy as np

D_MODEL = 4096
BATCH = 1
SEQ = 16384
DEPTH = 4

N_META = 16
N_MIXERS = 2
N_MLA_LAYERS = (DEPTH + 1) // 2
N_GDN_LAYERS = DEPTH // 2
NORM_EPS = 1e-6

MLA_HEADS = D_MODEL // 64
MLA_Q_RANK = 1536
MLA_KV_RANK = 512
MLA_NOPE_DIM = 128
MLA_ROPE_DIM = 64
MLA_V_DIM = 128
MLA_QK_DIM = MLA_NOPE_DIM + MLA_ROPE_DIM
ROPE_THETA = 10000.0
Q_BLOCK = 128

GDN_HEADS = D_MODEL // 128
GDN_K_DIM = 128
GDN_V_DIM = 128
GDN_KEY_WIDTH = GDN_HEADS * GDN_K_DIM
GDN_VALUE_WIDTH = GDN_HEADS * GDN_V_DIM
GDN_CONV = 4
GDN_CHUNK = 64

D_FF = 4 * D_MODEL

kernel_name = "hybrid_mla_gdn_sqrelu_sandwich_meta"


def _rmsnorm(x, w):
    x32 = x.astype(jnp.float32)
    y = x32 * lax.rsqrt(jnp.mean(x32 * x32, axis=-1, keepdims=True) + NORM_EPS)
    return (y * w.astype(jnp.float32)).astype(x.dtype)


def _l2norm(t):
    return t * lax.rsqrt(jnp.sum(t * t, axis=-1, keepdims=True) + NORM_EPS)


def _rope_tables(length):
    inv = ROPE_THETA ** (-jnp.arange(0, MLA_ROPE_DIM, 2, dtype=jnp.float32) / MLA_ROPE_DIM)
    ang = jnp.arange(length, dtype=jnp.float32)[:, None] * inv[None, :]
    return jnp.cos(ang), jnp.sin(ang)


def _apply_rope(x, cos, sin):
    x32 = x.astype(jnp.float32)
    half = MLA_ROPE_DIM // 2
    x1, x2 = x32[..., :half], x32[..., half:]
    c, s = cos[:, None, :], sin[:, None, :]
    return jnp.concatenate([x1 * c - x2 * s, x2 * c + x1 * s], axis=-1).astype(x.dtype)


def _mla(h, wq_a, q_norm, wq_b, wkv_a, kv_norm, wkv_b, wo, cos, sin):
    b, L, _ = h.shape
    cq = _rmsnorm(h @ wq_a, q_norm)
    q = (cq @ wq_b).reshape(b, L, MLA_HEADS, MLA_QK_DIM)
    q_nope = q[..., :MLA_NOPE_DIM]
    q_pe = _apply_rope(q[..., MLA_NOPE_DIM:], cos, sin)
    kv_a = h @ wkv_a
    c_kv = _rmsnorm(kv_a[..., :MLA_KV_RANK], kv_norm)
    k_pe = _apply_rope(kv_a[..., None, MLA_KV_RANK:], cos, sin)[:, :, 0]
    kv = (c_kv @ wkv_b).reshape(b, L, MLA_HEADS, MLA_NOPE_DIM + MLA_V_DIM)
    k_nope, v = kv[..., :MLA_NOPE_DIM], kv[..., MLA_NOPE_DIM:]

    n_blocks = -(-L // Q_BLOCK)
    pad = n_blocks * Q_BLOCK - L

    def to_blocks(t):
        t = jnp.pad(t, ((0, 0), (0, pad), (0, 0), (0, 0)))
        return jnp.moveaxis(t.reshape(b, n_blocks, Q_BLOCK, *t.shape[2:]), 1, 0)

    starts = jnp.arange(n_blocks, dtype=jnp.int32) * Q_BLOCK
    key_pos = jnp.arange(L, dtype=jnp.int32)
    scale = MLA_QK_DIM ** -0.5

    def attend(args):
        qn, qp, start = args
        s = (jnp.einsum('bqhd,bkhd->bhqk', qn, k_nope)
             + jnp.einsum('bqhr,bkr->bhqk', qp, k_pe)).astype(jnp.float32) * scale
        q_pos = start + jnp.arange(Q_BLOCK, dtype=jnp.int32)
        s = jnp.where(key_pos[None, :] <= q_pos[:, None], s, -jnp.inf)
        p = jax.nn.softmax(s, axis=-1).astype(v.dtype)
        return jnp.einsum('bhqk,bkhd->bqhd', p, v)

    o = lax.map(attend, (to_blocks(q_nope), to_blocks(q_pe), starts))
    o = jnp.moveaxis(o, 0, 1).reshape(b, n_blocks * Q_BLOCK, MLA_HEADS * MLA_V_DIM)[:, :L]
    return o @ wo


def _gated_delta_rule(q, k, v, g, beta):
    b, L, H, dk = q.shape
    dv = v.shape[-1]
    C = GDN_CHUNK
    lead = C - N_META

    def front_pad(t):
        return jnp.pad(t, ((0, 0), (lead, 0)) + ((0, 0),) * (t.ndim - 2))

    q, k, v, g, beta = (front_pad(t) for t in (q, k, v, g, beta))
    n = (L + lead) // C

    def chunks(t):
        return jnp.moveaxis(t.reshape(b, n, C, H, *t.shape[3:]), 3, 1)

    q, k, v, g, beta = (chunks(t) for t in (q, k, v, g, beta))
    gc = jnp.cumsum(g, axis=-1)
    idx = jnp.arange(C)
    causal = idx[:, None] >= idx[None, :]
    decay = jnp.exp(jnp.where(causal, gc[..., :, None] - gc[..., None, :], -jnp.inf))
    kb = k * beta[..., None]
    a_mat = jnp.einsum('bhncd,bhnsd->bhncs', kb, k) * decay
    rhs = jnp.concatenate([v * beta[..., None], kb * jnp.exp(gc)[..., None]], axis=-1)
    uw = lax.linalg.triangular_solve(a_mat, rhs, left_side=True, lower=True, unit_diagonal=True)
    u, w = uw[..., :dv], uw[..., dv:]
    attn_intra = jnp.einsum('bhncd,bhnsd->bhncs', q, k) * decay
    q_dec = q * jnp.exp(gc)[..., None]
    g_last = gc[..., -1]
    k_dec = k * jnp.exp(g_last[..., None] - gc)[..., None]

    def step(S, xs):
        u_c, w_c, qd_c, kd_c, at_c, gl_c = xs
        v_new = u_c - jnp.einsum('bhcd,bhde->bhce', w_c, S)
        o_c = jnp.einsum('bhcd,bhde->bhce', qd_c, S) + jnp.einsum('bhcs,bhse->bhce', at_c, v_new)
        S = S * jnp.exp(gl_c)[..., None, None] + jnp.einsum('bhcd,bhce->bhde', kd_c, v_new)
        return S, o_c

    xs = tuple(jnp.moveaxis(t, 2, 0) for t in (u, w, q_dec, k_dec, attn_intra, g_last))
    S0 = jnp.zeros((b, H, dk, dv), jnp.float32)
    _, o = lax.scan(step, S0, xs)
    o = jnp.moveaxis(jnp.moveaxis(o, 0, 2), 1, 3).reshape(b, n * C, H, dv)
    return o[:, lead:]


def _gdn(h, w_qkvz, w_ba, conv_w, a_log, dt_bias, o_norm, wo):
    b, L, _ = h.shape
    KW, VW = GDN_KEY_WIDTH, GDN_VALUE_WIDTH
    proj = h @ w_qkvz
    qkv = proj[..., :2 * KW + VW]
    z = proj[..., 2 * KW + VW:]
    qkv = lax.conv_general_dilated(qkv, conv_w[:, None, :].astype(qkv.dtype), window_strides=(1,),
                                   padding=((GDN_CONV - 1, 0),),
                                   dimension_numbers=('NWC', 'WIO', 'NWC'),
                                   feature_group_count=qkv.shape[-1])
    qkv = jax.nn.silu(qkv).astype(jnp.float32)
    q = _l2norm(qkv[..., :KW].reshape(b, L, GDN_HEADS, GDN_K_DIM)) * (GDN_K_DIM ** -0.5)
    k = _l2norm(qkv[..., KW:2 * KW].reshape(b, L, GDN_HEADS, GDN_K_DIM))
    v = qkv[..., 2 * KW:].reshape(b, L, GDN_HEADS, GDN_V_DIM)
    ba = (h @ w_ba).astype(jnp.float32)
    beta = jax.nn.sigmoid(ba[..., :GDN_HEADS])
    g = -jnp.exp(a_log.astype(jnp.float32)) * jax.nn.softplus(ba[..., GDN_HEADS:] + dt_bias.astype(jnp.float32))
    o = _gated_delta_rule(q, k, v, g, beta)
    o = _rmsnorm(o, o_norm) * jax.nn.silu(z.reshape(b, L, GDN_HEADS, GDN_V_DIM).astype(jnp.float32))
    return o.reshape(b, L, VW).astype(h.dtype) @ wo


def _sqrelu_mlp(h, w_up, w_down):
    return jnp.square(jax.nn.relu(h @ w_up)) @ w_down


def _dense(key, shape, fan_in):
    return jax.random.normal(key, shape, jnp.float32) * (fan_in ** -0.5)


def _gain(key, shape):
    return 1.0 + 0.02 * jax.random.normal(key, shape, jnp.float32)


def setup_inputs(seed: int = 0) -> dict:
    key = jax.random.key(seed)
    ks = jax.random.split(key, 20)
    KW, VW = GDN_KEY_WIDTH, GDN_VALUE_WIDTH
    x = jax.random.normal(ks[0], (BATCH, SEQ, D_MODEL), jnp.float32)
    meta_tokens = jax.random.normal(ks[1], (N_META, D_MODEL), jnp.float32)
    norm_gains = _gain(ks[2], (DEPTH, 4, D_MODEL))
    mla_wq_a = _dense(ks[3], (N_MLA_LAYERS, D_MODEL, MLA_Q_RANK), D_MODEL)
    mla_q_norm = _gain(ks[4], (N_MLA_LAYERS, MLA_Q_RANK))
    mla_wq_b = _dense(ks[5], (N_MLA_LAYERS, MLA_Q_RANK, MLA_HEADS * MLA_QK_DIM), MLA_Q_RANK)
    mla_wkv_a = _dense(ks[6], (N_MLA_LAYERS, D_MODEL, MLA_KV_RANK + MLA_ROPE_DIM), D_MODEL)
    mla_kv_norm = _gain(ks[7], (N_MLA_LAYERS, MLA_KV_RANK))
    mla_wkv_b = _dense(ks[8], (N_MLA_LAYERS, MLA_KV_RANK, MLA_HEADS * (MLA_NOPE_DIM + MLA_V_DIM)), MLA_KV_RANK)
    mla_wo = _dense(ks[9], (N_MLA_LAYERS, MLA_HEADS * MLA_V_DIM, D_MODEL), MLA_HEADS * MLA_V_DIM)
    gdn_w_qkvz = _dense(ks[10], (N_GDN_LAYERS, D_MODEL, 2 * KW + 2 * VW), D_MODEL)
    gdn_w_ba = _dense(ks[11], (N_GDN_LAYERS, D_MODEL, 2 * GDN_HEADS), D_MODEL)
    gdn_conv_w = _dense(ks[12], (N_GDN_LAYERS, GDN_CONV, 2 * KW + VW), GDN_CONV)
    gdn_a_log = jnp.log(jax.random.uniform(ks[13], (N_GDN_LAYERS, GDN_HEADS), jnp.float32, 1.0, 16.0))
    dt = jnp.exp(jax.random.uniform(ks[14], (N_GDN_LAYERS, GDN_HEADS), jnp.float32,
                                    math.log(1e-3), math.log(1e-1)))
    gdn_dt_bias = dt + jnp.log(-jnp.expm1(-dt))
    gdn_o_norm = _gain(ks[15], (N_GDN_LAYERS, GDN_V_DIM))
    gdn_wo = _dense(ks[16], (N_GDN_LAYERS, VW, D_MODEL), VW)
    mlp_w_up = _dense(ks[17], (DEPTH, D_MODEL, D_FF), D_MODEL)
    mlp_w_down = _dense(ks[18], (DEPTH, D_FF, D_MODEL), D_FF)
    return {"x": x, "meta_tokens": meta_tokens, "norm_gains": norm_gains,
            "mla_wq_a": mla_wq_a, "mla_q_norm": mla_q_norm, "mla_wq_b": mla_wq_b,
            "mla_wkv_a": mla_wkv_a, "mla_kv_norm": mla_kv_norm, "mla_wkv_b": mla_wkv_b,
            "mla_wo": mla_wo, "gdn_w_qkvz": gdn_w_qkvz, "gdn_w_ba": gdn_w_ba,
            "gdn_conv_w": gdn_conv_w, "gdn_a_log": gdn_a_log, "gdn_dt_bias": gdn_dt_bias,
            "gdn_o_norm": gdn_o_norm, "gdn_wo": gdn_wo,
            "mlp_w_up": mlp_w_up, "mlp_w_down": mlp_w_down}


def reference(x, meta_tokens, norm_gains, mla_wq_a, mla_q_norm, mla_wq_b, mla_wkv_a, mla_kv_norm,
              mla_wkv_b, mla_wo, gdn_w_qkvz, gdn_w_ba, gdn_conv_w, gdn_a_log, gdn_dt_bias,
              gdn_o_norm, gdn_wo, mlp_w_up, mlp_w_down):
    b = x.shape[0]
    meta = jnp.broadcast_to(meta_tokens.astype(x.dtype)[None], (b, N_META, D_MODEL))
    h = jnp.concatenate([meta, x], axis=1)
    cos, sin = _rope_tables(h.shape[1])
    for i in range(DEPTH):
        gains = norm_gains[i]
        hn = _rmsnorm(h, gains[0])
        j = i // N_MIXERS
        if i % N_MIXERS == 0:
            mix = _mla(hn, mla_wq_a[j], mla_q_norm[j], mla_wq_b[j], mla_wkv_a[j], mla_kv_norm[j],
                       mla_wkv_b[j], mla_wo[j], cos, sin)
        else:
            mix = _gdn(hn, gdn_w_qkvz[j], gdn_w_ba[j], gdn_conv_w[j], gdn_a_log[j], gdn_dt_bias[j],
                       gdn_o_norm[j], gdn_wo[j])
        h = h + _rmsnorm(mix, gains[1])
        ff = _sqrelu_mlp(_rmsnorm(h, gains[2]), mlp_w_up[i], mlp_w_down[i])
        h = h + _rmsnorm(ff, gains[3])
    return h[:, N_META:]
```

```python
import functools
import math

import jax
import jax.numpy as jnp
from jax import lax
from jax.experimental import pallas as pl
from jax.experimental.pallas import tpu as pltpu

NORM_EPS = 1e-6
ROPE_THETA = 10000.0
NOPE_DIM = 128
ROPE_DIM = 64
V_DIM = 128
QK_PAD = 256
GDN_DIM = 128
GDN_CONV = 4
CHUNK = 64
SUB = 16
LANE = 128
_ROW_ALIGN = 256
_VMEM_CAP = 56 * 1024 * 1024

_F32 = jnp.float32
_BF16 = jnp.bfloat16
_NT = (((1,), (1,)), ((), ()))
_TN = (((0,), (0,)), ((), ()))


def _pick(n, candidates):
    for c in candidates:
        if n % c == 0:
            return c
    raise ValueError(f"no tile in {candidates} divides {n}")


def _cparams(sem, vmem_bytes):
    limit = int(min(max(vmem_bytes * 5 // 4 + (4 << 20), 32 << 20), _VMEM_CAP))
    return pltpu.CompilerParams(dimension_semantics=sem, vmem_limit_bytes=limit)


def _bdot(a, b, dims=None):
    a = a.astype(_BF16)
    b = b.astype(_BF16)
    if dims is None:
        return jnp.dot(a, b, preferred_element_type=_F32)
    return lax.dot_general(a, b, dims, preferred_element_type=_F32)


def _split3(x):
    x1 = x.astype(_BF16)
    r1 = x - x1.astype(_F32)
    x2 = r1.astype(_BF16)
    r2 = r1 - x2.astype(_F32)
    x3 = r2.astype(_BF16)
    return x1, x2, x3


def _sigmoid(x):
    return 1.0 / (1.0 + jnp.exp(-x))


def _rms(x, w):
    return x * lax.rsqrt(jnp.mean(x * x, axis=-1, keepdims=True) + NORM_EPS) * w


def _prenorm_kernel(h_ref, g_ref, o_ref):
    o_ref[...] = _rms(h_ref[...], g_ref[...]).astype(o_ref.dtype)


def _prenorm(h, gain, tr):
    L, D = h.shape
    return pl.pallas_call(
        _prenorm_kernel,
        grid=(L // tr,),
        in_specs=[pl.BlockSpec((tr, D), lambda i: (i, 0)),
                  pl.BlockSpec((1, D), lambda i: (0, 0))],
        out_specs=pl.BlockSpec((tr, D), lambda i: (i, 0)),
        out_shape=jax.ShapeDtypeStruct((L, D), _BF16),
        compiler_params=_cparams(("parallel",), 2 * tr * D * 6),
        name="prenorm",
    )(h, gain.reshape(1, D))


def _add_norm_kernel(h_ref, m_ref, gp_ref, gn_ref, ho_ref, hn_ref):
    h = h_ref[...] + _rms(m_ref[...], gp_ref[...])
    ho_ref[...] = h
    hn_ref[...] = _rms(h, gn_ref[...]).astype(hn_ref.dtype)


def _add_kernel(h_ref, m_ref, gp_ref, ho_ref):
    ho_ref[...] = h_ref[...] + _rms(m_ref[...], gp_ref[...])


def _add_norm(h, mix, g_post, g_next, tr):
    L, D = h.shape
    row = pl.BlockSpec((tr, D), lambda i: (i, 0))
    vec = pl.BlockSpec((1, D), lambda i: (0, 0))
    if g_next is None:
        return pl.pallas_call(
            _add_kernel, grid=(L // tr,),
            in_specs=[row, row, vec], out_specs=row,
            out_shape=jax.ShapeDtypeStruct((L, D), _F32),
            compiler_params=_cparams(("parallel",), 2 * tr * D * 12),
            name="add_post",
        )(h, mix, g_post.reshape(1, D)), None
    return pl.pallas_call(
        _add_norm_kernel, grid=(L // tr,),
        in_specs=[row, row, vec, vec], out_specs=[row, row],
        out_shape=[jax.ShapeDtypeStruct((L, D), _F32), jax.ShapeDtypeStruct((L, D), _BF16)],
        compiler_params=_cparams(("parallel",), 2 * tr * D * 14),
        name="add_norm",
    )(h, mix, g_post.reshape(1, D), g_next.reshape(1, D))


def _mm_kernel(*refs, nk, n_extra, n_out, epilogue):
    a_ref, b_ref = refs[0], refs[1]
    extra = refs[2:2 + n_extra]
    outs = refs[2 + n_extra:2 + n_extra + n_out]
    if nk == 1:
        epilogue(jnp.dot(a_ref[...], b_ref[...], preferred_element_type=_F32), extra, outs)
        return
    acc_ref = refs[-1]
    k = pl.program_id(2)

    @pl.when(k == 0)
    def _():
        acc_ref[...] = jnp.zeros_like(acc_ref)

    acc_ref[...] += jnp.dot(a_ref[...], b_ref[...], preferred_element_type=_F32)

    @pl.when(k == nk - 1)
    def _():
        epilogue(acc_ref[...], extra, outs)


def _matmul(a, b, *, tm, tn, tk, epilogue, out_shapes, out_specs, extras=(), extra_specs=(),
            name="matmul"):
    M, K = a.shape
    N = b.shape[1]
    nk = K // tk
    kern = functools.partial(_mm_kernel, nk=nk, n_extra=len(extras), n_out=len(out_shapes),
                             epilogue=epilogue)
    out_bytes = sum(math.prod(s.block_shape) * jnp.dtype(o.dtype).itemsize
                    for s, o in zip(out_specs, out_shapes))
    vmem = 2 * (tm * tk * 2 + tk * tn * 2 + out_bytes) + 2 * tm * tn * 4
    vmem += 2 * sum(math.prod(s.block_shape) * 4 for s in extra_specs)
    return pl.pallas_call(
        kern,
        grid=(M // tm, N // tn, nk),
        in_specs=[pl.BlockSpec((tm, tk), lambda i, j, k: (i, k)),
                  pl.BlockSpec((tk, tn), lambda i, j, k: (k, j))] + list(extra_specs),
        out_specs=list(out_specs),
        out_shape=list(out_shapes),
        scratch_shapes=[pltpu.VMEM((tm, tn), _F32)] if nk > 1 else [],
        compiler_params=_cparams(("parallel", "parallel", "arbitrary"), vmem),
        name=name,
    )(a, b, *extras)


def _ep_store(acc, extra, outs):
    outs[0][...] = acc.astype(outs[0].dtype)


def _ep_relu2(acc, extra, outs):
    r = jnp.maximum(acc, 0.0)
    outs[0][...] = (r * r).astype(outs[0].dtype)


_FULL_K_MAX = 4096


def _mm_tiles(M, K, N):
    tn = _pick(N, (1024, 512, 256, 128))
    if K <= _FULL_K_MAX:
        return _pick(M, (640, 512, 256)), tn, K
    return _pick(M, (1280, 1024, 512, 256)), tn, _pick(K, (2048, 1024, 512, 256, 128))


def _plain_matmul(a, b, out_dtype, *, epilogue=_ep_store, name="matmul"):
    M, N = a.shape[0], b.shape[1]
    tm, tn, tk = _mm_tiles(M, a.shape[1], N)
    return _matmul(a, b, tm=tm, tn=tn, tk=tk, epilogue=epilogue,
                   out_shapes=[jax.ShapeDtypeStruct((M, N), out_dtype)],
                   out_specs=[pl.BlockSpec((tm, tn), lambda i, j, k: (i, j))], name=name)[0]


def _rope128(x, cos, sin):
    lane = lax.broadcasted_iota(jnp.int32, x.shape, 1)
    half = ROPE_DIM // 2
    partner = jnp.where(lane % ROPE_DIM < half,
                        pltpu.roll(x, LANE - half, 1), pltpu.roll(x, half, 1))
    return x * cos + partner * sin


def _ep_mla_a(acc, extra, outs, *, q_rank, kv_rank):
    qn_ref, kvn_ref, cos_ref, sin_ref = extra
    cq_ref, ckv_ref, kpe_ref = outs
    cq_ref[...] = _rms(acc[:, :q_rank], qn_ref[...]).astype(cq_ref.dtype)
    ckv_ref[...] = _rms(acc[:, q_rank:q_rank + kv_rank], kvn_ref[...]).astype(ckv_ref.dtype)
    kpe = acc[:, q_rank + kv_rank:]
    kpe_ref[...] = _rope128(kpe, cos_ref[...], sin_ref[...]).astype(kpe_ref.dtype)


def _ep_mla_q(acc, extra, outs, *, heads, scale):
    cos_ref, sin_ref = extra
    o_ref = outs[0]
    cos = cos_ref[...]
    sin = sin_ref[...]
    for hh in range(heads):
        base = hh * QK_PAD
        o_ref[:, base:base + NOPE_DIM] = (acc[:, base:base + NOPE_DIM] * scale).astype(o_ref.dtype)
        pe = _rope128(acc[:, base + NOPE_DIM:base + QK_PAD], cos, sin)
        o_ref[:, base + NOPE_DIM:base + QK_PAD] = (pe * scale).astype(o_ref.dtype)


def _kv_kernel(c_ref, wk_ref, wv_ref, kpe_ref, k_ref, v_ref, *, heads):
    c = c_ref[...]
    kn = jnp.dot(c, wk_ref[...], preferred_element_type=_F32)
    v_ref[...] = jnp.dot(c, wv_ref[...], preferred_element_type=_F32).astype(v_ref.dtype)
    kpe = kpe_ref[...]
    for hh in range(heads):
        k_ref[:, hh * QK_PAD:hh * QK_PAD + NOPE_DIM] = (
            kn[:, hh * NOPE_DIM:(hh + 1) * NOPE_DIM].astype(k_ref.dtype))
        k_ref[:, hh * QK_PAD + NOPE_DIM:(hh + 1) * QK_PAD] = kpe


def _kv_proj(ckv, wk, wv, kpe, tm, heads_per_step):
    L, R = ckv.shape
    H = wk.shape[1] // NOPE_DIM
    g = heads_per_step
    kern = functools.partial(_kv_kernel, heads=g)
    vmem = 2 * (tm * R * 2 + 2 * R * g * 128 * 2 + tm * 128 * 2 + tm * g * 384 * 2) + tm * g * 256 * 8
    return pl.pallas_call(
        kern,
        grid=(L // tm, H // g),
        in_specs=[pl.BlockSpec((tm, R), lambda i, j: (i, 0)),
                  pl.BlockSpec((R, g * NOPE_DIM), lambda i, j: (0, j)),
                  pl.BlockSpec((R, g * V_DIM), lambda i, j: (0, j)),
                  pl.BlockSpec((tm, LANE), lambda i, j: (i, 0))],
        out_specs=[pl.BlockSpec((tm, g * QK_PAD), lambda i, j: (i, j)),
                   pl.BlockSpec((tm, g * V_DIM), lambda i, j: (i, j))],
        out_shape=[jax.ShapeDtypeStruct((L, H * QK_PAD), _BF16),
                   jax.ShapeDtypeStruct((L, H * V_DIM), _BF16)],
        compiler_params=_cparams(("parallel", "parallel"), vmem),
        name="mla_kv",
    )(ckv, wk, wv, kpe)


def _attn_kernel(q_ref, k_ref, v_ref, o_ref, m_sc, l_sc, acc_sc, *, tq, tk):
    qi = pl.program_id(1)
    m_sc[...] = jnp.full(m_sc.shape, -jnp.inf, _F32)
    l_sc[...] = jnp.zeros(l_sc.shape, _F32)
    acc_sc[...] = jnp.zeros(acc_sc.shape, _F32)

    def step(kstart, row0, masked):
        rows = pl.ds(row0, tq - row0)
        q = q_ref[rows, :]
        k = k_ref[pl.ds(kstart, tk), :]
        s = lax.dot_general(q, k, _NT, preferred_element_type=_F32)
        if masked:
            qpos = qi * tq + row0 + lax.broadcasted_iota(jnp.int32, s.shape, 0)
            kpos = kstart + lax.broadcasted_iota(jnp.int32, s.shape, 1)
            s = jnp.where(kpos <= qpos, s, -jnp.inf)
        m_prev = m_sc[rows, :]
        m_new = jnp.maximum(m_prev, jnp.max(s, axis=-1, keepdims=True))
        alpha = jnp.exp(m_prev - m_new)
        p = jnp.exp(s - m_new)
        l_sc[rows, :] = alpha * l_sc[rows, :] + jnp.sum(p, axis=-1, keepdims=True)
        pv = jnp.dot(p.astype(_BF16), v_ref[pl.ds(kstart, tk), :], preferred_element_type=_F32)
        acc_sc[rows, :] = alpha * acc_sc[rows, :] + pv
        m_sc[rows, :] = m_new

    def full_step(j, carry):
        step(pl.multiple_of(j * tk, tk), 0, False)
        return carry

    lax.fori_loop(0, qi * (tq // tk), full_step, 0)
    for d in range(tq // tk):
        step(pl.multiple_of(qi * tq + d * tk, tk), d * tk, True)
    o_ref[...] = (acc_sc[...] / l_sc[...]).astype(o_ref.dtype)


def _attention(q, k, v, tq, tk):
    L = q.shape[0]
    H = q.shape[1] // QK_PAD
    kern = functools.partial(_attn_kernel, tq=tq, tk=tk)
    vmem = (2 * (L * QK_PAD * 2 + L * V_DIM * 2 + tq * QK_PAD * 2 + tq * V_DIM * 2)
            + 3 * tq * LANE * 4 + 6 * tq * tk * 4)
    return pl.pallas_call(
        kern,
        grid=(H, L // tq),
        in_specs=[pl.BlockSpec((tq, QK_PAD), lambda h, i: (i, h)),
                  pl.BlockSpec((L, QK_PAD), lambda h, i: (0, h)),
                  pl.BlockSpec((L, V_DIM), lambda h, i: (0, h))],
        out_specs=pl.BlockSpec((tq, V_DIM), lambda h, i: (i, h)),
        out_shape=jax.ShapeDtypeStruct((L, H * V_DIM), _BF16),
        scratch_shapes=[pltpu.VMEM((tq, 1), _F32), pltpu.VMEM((tq, 1), _F32),
                        pltpu.VMEM((tq, V_DIM), _F32)],
        compiler_params=_cparams(("parallel", "arbitrary"), vmem),
        name="mla_attention",
    )(q, k, v)


def _mla(hn, w_a, q_norm, kv_norm, w_qb, w_kb, w_vb, w_o, cos, sin, tiles):
    L, D = hn.shape
    q_rank = q_norm.shape[-1]
    kv_rank = kv_norm.shape[-1]
    heads = w_kb.shape[1] // NOPE_DIM
    tm = _pick(L, (640, 512, 256))
    row_tab = pl.BlockSpec((tm, LANE), lambda i, j, k: (i, 0))
    n_a = w_a.shape[1]
    cq, ckv, kpe = _matmul(
        hn, w_a, tm=tm, tn=n_a, tk=_pick(D, (1024, 512, 256, 128)),
        epilogue=functools.partial(_ep_mla_a, q_rank=q_rank, kv_rank=kv_rank),
        out_shapes=[jax.ShapeDtypeStruct((L, q_rank), _BF16),
                    jax.ShapeDtypeStruct((L, kv_rank), _BF16),
                    jax.ShapeDtypeStruct((L, LANE), _BF16)],
        out_specs=[pl.BlockSpec((tm, q_rank), lambda i, j, k: (i, 0)),
                   pl.BlockSpec((tm, kv_rank), lambda i, j, k: (i, 0)),
                   pl.BlockSpec((tm, LANE), lambda i, j, k: (i, 0))],
        extras=(q_norm.reshape(1, -1), kv_norm.reshape(1, -1), cos, sin),
        extra_specs=(pl.BlockSpec((1, q_rank), lambda i, j, k: (0, 0)),
                     pl.BlockSpec((1, kv_rank), lambda i, j, k: (0, 0)), row_tab, row_tab),
        name="mla_latents")
    g = _pick(heads, (4, 2, 1))
    q = _matmul(
        cq, w_qb, tm=tm, tn=g * QK_PAD, tk=q_rank,
        epilogue=functools.partial(_ep_mla_q, heads=g, scale=(NOPE_DIM + ROPE_DIM) ** -0.5),
        out_shapes=[jax.ShapeDtypeStruct((L, heads * QK_PAD), _BF16)],
        out_specs=[pl.BlockSpec((tm, g * QK_PAD), lambda i, j, k: (i, j))],
        extras=(cos, sin), extra_specs=(row_tab, row_tab), name="mla_q")[0]
    k, v = _kv_proj(ckv, w_kb, w_vb, kpe, tm, g)
    o = _attention(q, k, v, tiles["tq"], tiles["tk"])
    return _plain_matmul(o, w_o, _F32, name="mla_out")


def _chunk_solve(a_mat, rhs, row, col):
    same = (row // SUB) == (col // SUB)
    eye = (row == col).astype(_F32)
    p = jnp.where(same, -a_mat, 0.0)
    t = eye + p
    for _ in range(int(math.log2(SUB)) - 1):
        p = _bdot(p, p)
        t = t + _bdot(t, p)
    m = -_bdot(t, jnp.where(same, 0.0, a_mat))
    x = _bdot(t, rhs)
    levels = int(math.log2(CHUNK // SUB))
    for lvl in range(levels):
        x = x + _bdot(m, x)
        if lvl + 1 < levels:
            m = _bdot(m, m)
    return x


def _gdn_kernel(pq_ref, pk_ref, pv_ref, pz_ref, wq_ref, wk_ref, wv_ref, ba_ref, gp_ref, on_ref,
                o_ref, xq_sc, xk_sc, xv_sc, s_sc, *, rows):
    h = pl.program_id(0)
    c = pl.program_id(1)
    tail = GDN_CONV - 1

    @pl.when(c == 0)
    def _():
        zero8 = jnp.zeros((8, GDN_DIM), _F32)
        xq_sc[0:8, :] = zero8
        xk_sc[0:8, :] = zero8
        xv_sc[0:8, :] = zero8
        s_sc[...] = jnp.zeros(s_sc.shape, _F32)

    def conv_silu(p_ref, w_ref, x_sc):
        x_sc[8:8 + rows, :] = p_ref[...].astype(_F32)
        w = w_ref[...]
        y = w[tail:tail + 1, :] * x_sc[8:8 + rows, :]
        for j in range(tail):
            y = y + w[j:j + 1, :] * x_sc[8 - tail + j:8 - tail + j + rows, :]
        x_sc[0:8, :] = x_sc[rows:rows + 8, :]
        return y * _sigmoid(y)

    def l2norm(t):
        return t * lax.rsqrt(jnp.sum(t * t, axis=-1, keepdims=True) + NORM_EPS)

    q = l2norm(conv_silu(pq_ref, wq_ref, xq_sc)) * (GDN_DIM ** -0.5)
    k = l2norm(conv_silu(pk_ref, wk_ref, xk_sc))
    v = conv_silu(pv_ref, wv_ref, xv_sc)

    ba = ba_ref[...]
    n_heads = pl.num_programs(0)
    lane = lax.broadcasted_iota(jnp.int32, ba.shape, 1)
    a_log = gp_ref[0:1, :]
    dt_bias = gp_ref[1:2, :]
    xa = ba + dt_bias
    softplus = jnp.maximum(xa, 0.0) + jnp.log(1.0 + jnp.exp(-jnp.abs(xa)))
    g_all = -jnp.exp(a_log) * softplus
    beta = jnp.sum(jnp.where(lane == h, _sigmoid(ba), 0.0), axis=-1, keepdims=True)
    g = jnp.sum(jnp.where(lane == n_heads + h, g_all, 0.0), axis=-1, keepdims=True)

    rr = lax.broadcasted_iota(jnp.int32, (rows, rows), 0)
    cc = lax.broadcasted_iota(jnp.int32, (rows, rows), 1)
    tri = jnp.where((rr // CHUNK == cc // CHUNK) & (rr >= cc), 1.0, 0.0).astype(_BF16)
    gc = jnp.zeros((rows, LANE), _F32)
    for part in _split3(jnp.broadcast_to(g, (rows, LANE))):
        gc = gc + jnp.dot(tri, part, preferred_element_type=_F32)
    lane0 = lax.broadcasted_iota(jnp.int32, (rows, LANE), 1) == 0
    ones16 = jnp.ones((16, LANE), _BF16)
    gc_row = jnp.zeros((16, rows), _F32)
    for part in _split3(jnp.where(lane0, gc, 0.0)):
        gc_row = gc_row + lax.dot_general(ones16, part, _NT, preferred_element_type=_F32)

    row = lax.broadcasted_iota(jnp.int32, (CHUNK, CHUNK), 0)
    col = lax.broadcasted_iota(jnp.int32, (CHUNK, CHUNK), 1)
    onorm = on_ref[...]
    for ci in range(rows // CHUNK):
        sl = slice(ci * CHUNK, (ci + 1) * CHUNK)
        qc, kc, vc = q[sl], k[sl], v[sl]
        bc = beta[sl]
        gcb = gc[sl]
        diff = gcb[:, :CHUNK] - gc_row[0:1, sl]
        dec = jnp.exp(jnp.where(row >= col, diff, -jnp.inf))
        egc = jnp.exp(gcb)
        g_last = gcb[CHUNK - 1:CHUNK, :]
        kb = kc * bc
        a_mat = jnp.where(row > col, _bdot(kb, kc, _NT) * dec, 0.0)
        attn = _bdot(qc, kc, _NT) * dec
        uw = _chunk_solve(a_mat, jnp.concatenate([vc * bc, kb * egc], axis=1), row, col)
        u, w = uw[:, :GDN_DIM], uw[:, GDN_DIM:]
        s = s_sc[...]
        v_new = u - _bdot(w, s)
        o = _bdot(qc * egc, s) + _bdot(attn, v_new)
        kd = kc * jnp.exp(g_last - gcb)
        s_sc[...] = s * jnp.exp(g_last) + _bdot(kd, v_new, _TN)
        z = pz_ref[sl, :].astype(_F32)
        o_ref[sl, :] = (_rms(o, onorm) * (z * _sigmoid(z))).astype(o_ref.dtype)


def _gdn_mixer(proj, ba, conv_w, gate_params, o_norm, rows):
    L = proj.shape[0]
    H = proj.shape[1] // (4 * GDN_DIM)
    kern = functools.partial(_gdn_kernel, rows=rows)

    def col_block(off):
        return pl.BlockSpec((rows, GDN_DIM), lambda h, c: (c, off + h))

    def w_block(off):
        return pl.BlockSpec((GDN_CONV, GDN_DIM), lambda h, c: (0, off + h))

    itemsize = jnp.dtype(proj.dtype).itemsize
    vmem = 2 * (4 * rows * GDN_DIM * itemsize + rows * LANE * 4 + rows * GDN_DIM * 2) + (8 << 20)
    return pl.pallas_call(
        kern,
        grid=(H, L // rows),
        in_specs=[col_block(0), col_block(H), col_block(2 * H), col_block(3 * H),
                  w_block(0), w_block(H), w_block(2 * H),
                  pl.BlockSpec((rows, LANE), lambda h, c: (c, 0)),
                  pl.BlockSpec((8, LANE), lambda h, c: (0, 0)),
                  pl.BlockSpec((1, GDN_DIM), lambda h, c: (0, 0))],
        out_specs=pl.BlockSpec((rows, GDN_DIM), lambda h, c: (c, h)),
        out_shape=jax.ShapeDtypeStruct((L, H * GDN_DIM), _BF16),
        scratch_shapes=[pltpu.VMEM((rows + 8, GDN_DIM), _F32)] * 3
                       + [pltpu.VMEM((GDN_DIM, GDN_DIM), _F32)],
        compiler_params=_cparams(("parallel", "arbitrary"), vmem),
        name="gdn_delta_rule",
    )(proj, proj, proj, proj, conv_w, conv_w, conv_w, ba, gate_params, o_norm.reshape(1, -1))


def _gdn(hn, w_qkvz, w_ba, conv_w, gate_params, o_norm, w_o, tiles):
    L, D = hn.shape
    proj = _plain_matmul(hn, w_qkvz, _F32, name="gdn_qkvz")
    ba = _plain_matmul(hn, w_ba, _F32, name="gdn_ba")
    o = _gdn_mixer(proj, ba, conv_w, gate_params, o_norm, tiles["gdn_rows"])
    return _plain_matmul(o, w_o, _F32, name="gdn_out")


def _mlp(hn, w_up, w_down, tiles):
    up = _plain_matmul(hn, w_up, _BF16, epilogue=_ep_relu2, name="mlp_up")
    return _plain_matmul(up, w_down, _F32, name="mlp_down")


def _rope_tables(length):
    half = ROPE_DIM // 2
    inv = ROPE_THETA ** (-jnp.arange(0, ROPE_DIM, 2, dtype=_F32) / ROPE_DIM)
    ang = jnp.arange(length, dtype=_F32)[:, None] * inv[None, :]
    cos, sin = jnp.cos(ang), jnp.sin(ang)
    zeros = jnp.zeros((length, LANE - ROPE_DIM), _F32)
    return (jnp.concatenate([cos, cos, zeros], axis=1),
            jnp.concatenate([-sin, sin, zeros], axis=1))


def _mla_weights(wq_a, wq_b, wkv_a, wkv_b, wo, kv_rank):
    D = wq_a.shape[0]
    heads = wq_b.shape[1] // (NOPE_DIM + ROPE_DIM)
    w_a = jnp.concatenate(
        [wq_a, wkv_a, jnp.zeros((D, LANE - ROPE_DIM), wq_a.dtype)], axis=1).astype(_BF16)
    qb = wq_b.reshape(-1, heads, NOPE_DIM + ROPE_DIM)
    qb = jnp.pad(qb, ((0, 0), (0, 0), (0, QK_PAD - NOPE_DIM - ROPE_DIM)))
    w_qb = qb.reshape(-1, heads * QK_PAD).astype(_BF16)
    kvb = wkv_b.reshape(kv_rank, heads, NOPE_DIM + V_DIM)
    w_kb = kvb[:, :, :NOPE_DIM].reshape(kv_rank, heads * NOPE_DIM).astype(_BF16)
    w_vb = kvb[:, :, NOPE_DIM:].reshape(kv_rank, heads * V_DIM).astype(_BF16)
    return w_a, w_qb, w_kb, w_vb, wo.astype(_BF16)


def kernel(x, meta_tokens, norm_gains, mla_wq_a, mla_q_norm, mla_wq_b, mla_wkv_a, mla_kv_norm,
           mla_wkv_b, mla_wo, gdn_w_qkvz, gdn_w_ba, gdn_conv_w, gdn_a_log, gdn_dt_bias,
           gdn_o_norm, gdn_wo, mlp_w_up, mlp_w_down):
    batch, seq, D = x.shape
    n_meta = meta_tokens.shape[0]
    depth = norm_gains.shape[0]
    L = n_meta + seq
    Lp = -(-L // _ROW_ALIGN) * _ROW_ALIGN
    gdn_heads = gdn_w_ba.shape[-1] // 2
    assert 2 * gdn_heads <= LANE and gdn_w_qkvz.shape[-1] == 4 * gdn_heads * GDN_DIM
    tiles = {
        "tr": 256,
        "tq": _pick(Lp, (1280, 1024, 768, 512, 256)),
        "tk": 256,
        "gdn_rows": 256,
    }
    cos, sin = _rope_tables(Lp)
    outs = []
    for b in range(batch):
        h = jnp.concatenate([meta_tokens.astype(x.dtype), x[b],
                             jnp.zeros((Lp - L, D), x.dtype)], axis=0)
        hn = _prenorm(h, norm_gains[0, 0], tiles["tr"])
        for i in range(depth):
            gains = norm_gains[i]
            j = i // 2
            if i % 2 == 0:
                kv_rank = mla_kv_norm.shape[-1]
                w = _mla_weights(mla_wq_a[j], mla_wq_b[j], mla_wkv_a[j], mla_wkv_b[j], mla_wo[j],
                                 kv_rank)
                mix = _mla(hn, w[0], mla_q_norm[j], mla_kv_norm[j], w[1], w[2], w[3], w[4],
                           cos, sin, tiles)
            else:
                w_ba = jnp.pad(gdn_w_ba[j], ((0, 0), (0, LANE - 2 * gdn_heads))).astype(_BF16)
                gate_params = jnp.zeros((8, LANE), _F32)
                gate_params = gate_params.at[0, gdn_heads:2 * gdn_heads].set(gdn_a_log[j])
                gate_params = gate_params.at[1, gdn_heads:2 * gdn_heads].set(gdn_dt_bias[j])
                mix = _gdn(hn, gdn_w_qkvz[j].astype(_BF16), w_ba, gdn_conv_w[j], gate_params,
                           gdn_o_norm[j], gdn_wo[j].astype(_BF16), tiles)
            h, hn = _add_norm(h, mix, gains[1], gains[2], tiles["tr"])
            ff = _mlp(hn, mlp_w_up[i].astype(_BF16), mlp_w_down[i].astype(_BF16), tiles)
            g_next = norm_gains[i + 1, 0] if i + 1 < depth else None
            h, hn = _add_norm(h, ff, gains[3], g_next, tiles["tr"])
        outs.append(h[n_meta:L])
    return jnp.stack(outs, axis=0)
```

```python
import functools
import math

import jax
import jax.numpy as jnp
from jax import lax
from jax.experimental import pallas as pl
from jax.experimental.pallas import tpu as pltpu

NORM_EPS = 1e-6
ROPE_THETA = 10000.0
NOPE_DIM = 128
ROPE_DIM = 64
V_DIM = 128
QK_PAD = 256
GDN_DIM = 128
GDN_CONV = 4
CHUNK = 64
SUB = 16
LANE = 128
_ROW_ALIGN = 256
_VMEM_CAP = 56 * 1024 * 1024

_F32 = jnp.float32
_BF16 = jnp.bfloat16
_NT = (((1,), (1,)), ((), ()))
_TN = (((0,), (0,)), ((), ()))


def _pick(n, candidates):
    for c in candidates:
        if n % c == 0:
            return c
    raise ValueError(f"no tile in {candidates} divides {n}")


def _cparams(sem, vmem_bytes):
    limit = int(min(max(vmem_bytes * 5 // 4 + (4 << 20), 32 << 20), _VMEM_CAP))
    return pltpu.CompilerParams(dimension_semantics=sem, vmem_limit_bytes=limit)


def _bdot(a, b, dims=None):
    a = a.astype(_BF16)
    b = b.astype(_BF16)
    if dims is None:
        return jnp.dot(a, b, preferred_element_type=_F32)
    return lax.dot_general(a, b, dims, preferred_element_type=_F32)


def _split3(x):
    x1 = x.astype(_BF16)
    r1 = x - x1.astype(_F32)
    x2 = r1.astype(_BF16)
    r2 = r1 - x2.astype(_F32)
    x3 = r2.astype(_BF16)
    return x1, x2, x3


def _sigmoid(x):
    return 1.0 / (1.0 + jnp.exp(-x))


def _rms(x, w):
    return x * lax.rsqrt(jnp.mean(x * x, axis=-1, keepdims=True) + NORM_EPS) * w


def _prenorm_kernel(h_ref, g_ref, o_ref):
    o_ref[...] = _rms(h_ref[...], g_ref[...]).astype(o_ref.dtype)


def _prenorm(h, gain, tr):
    L, D = h.shape
    return pl.pallas_call(
        _prenorm_kernel,
        grid=(L // tr,),
        in_specs=[pl.BlockSpec((tr, D), lambda i: (i, 0)),
                  pl.BlockSpec((1, D), lambda i: (0, 0))],
        out_specs=pl.BlockSpec((tr, D), lambda i: (i, 0)),
        out_shape=jax.ShapeDtypeStruct((L, D), _BF16),
        compiler_params=_cparams(("parallel",), 2 * tr * D * 6),
        name="prenorm",
    )(h, gain.reshape(1, D))


def _add_norm_kernel(h_ref, m_ref, gp_ref, gn_ref, ho_ref, hn_ref):
    h = h_ref[...] + _rms(m_ref[...], gp_ref[...])
    ho_ref[...] = h
    hn_ref[...] = _rms(h, gn_ref[...]).astype(hn_ref.dtype)


def _add_kernel(h_ref, m_ref, gp_ref, ho_ref):
    ho_ref[...] = h_ref[...] + _rms(m_ref[...], gp_ref[...])


def _add_norm(h, mix, g_post, g_next, tr):
    L, D = h.shape
    row = pl.BlockSpec((tr, D), lambda i: (i, 0))
    vec = pl.BlockSpec((1, D), lambda i: (0, 0))
    if g_next is None:
        return pl.pallas_call(
            _add_kernel, grid=(L // tr,),
            in_specs=[row, row, vec], out_specs=row,
            out_shape=jax.ShapeDtypeStruct((L, D), _F32),
            compiler_params=_cparams(("parallel",), 2 * tr * D * 12),
            name="add_post",
        )(h, mix, g_post.reshape(1, D)), None
    return pl.pallas_call(
        _add_norm_kernel, grid=(L // tr,),
        in_specs=[row, row, vec, vec], out_specs=[row, row],
        out_shape=[jax.ShapeDtypeStruct((L, D), _F32), jax.ShapeDtypeStruct((L, D), _BF16)],
        compiler_params=_cparams(("parallel",), 2 * tr * D * 14),
        name="add_norm",
    )(h, mix, g_post.reshape(1, D), g_next.reshape(1, D))


def _mm_kernel(*refs, nk, n_extra, n_out, epilogue):
    a_ref, b_ref = refs[0], refs[1]
    extra = refs[2:2 + n_extra]
    outs = refs[2 + n_extra:2 + n_extra + n_out]
    if nk == 1:
        epilogue(jnp.dot(a_ref[...], b_ref[...], preferred_element_type=_F32), extra, outs)
        return
    acc_ref = refs[-1]
    k = pl.program_id(2)

    @pl.when(k == 0)
    def _():
        acc_ref[...] = jnp.zeros_like(acc_ref)

    acc_ref[...] += jnp.dot(a_ref[...], b_ref[...], preferred_element_type=_F32)

    @pl.when(k == nk - 1)
    def _():
        epilogue(acc_ref[...], extra, outs)


def _matmul(a, b, *, tm, tn, tk, epilogue, out_shapes, out_specs, extras=(), extra_specs=(),
            name="matmul"):
    M, K = a.shape
    N = b.shape[1]
    nk = K // tk
    kern = functools.partial(_mm_kernel, nk=nk, n_extra=len(extras), n_out=len(out_shapes),
                             epilogue=epilogue)
    out_bytes = sum(math.prod(s.block_shape) * jnp.dtype(o.dtype).itemsize
                    for s, o in zip(out_specs, out_shapes))
    vmem = 2 * (tm * tk * 2 + tk * tn * 2 + out_bytes) + 2 * tm * tn * 4
    vmem += 2 * sum(math.prod(s.block_shape) * 4 for s in extra_specs)
    return pl.pallas_call(
        kern,
        grid=(M // tm, N // tn, nk),
        in_specs=[pl.BlockSpec((tm, tk), lambda i, j, k: (i, k)),
                  pl.BlockSpec((tk, tn), lambda i, j, k: (k, j))] + list(extra_specs),
        out_specs=list(out_specs),
        out_shape=list(out_shapes),
        scratch_shapes=[pltpu.VMEM((tm, tn), _F32)] if nk > 1 else [],
        compiler_params=_cparams(("parallel", "parallel", "arbitrary"), vmem),
        name=name,
    )(a, b, *extras)


def _ep_store(acc, extra, outs):
    outs[0][...] = acc.astype(outs[0].dtype)


def _ep_relu2(acc, extra, outs):
    r = jnp.maximum(acc, 0.0)
    outs[0][...] = (r * r).astype(outs[0].dtype)


_FULL_K_MAX = 4096


def _mm_tiles(M, K, N):
    tn = _pick(N, (1024, 512, 256, 128))
    if K <= _FULL_K_MAX:
        return _pick(M, (640, 512, 256)), tn, K
    return _pick(M, (1280, 1024, 512, 256)), tn, _pick(K, (2048, 1024, 512, 256, 128))


def _plain_matmul(a, b, out_dtype, *, epilogue=_ep_store, name="matmul"):
    M, N = a.shape[0], b.shape[1]
    tm, tn, tk = _mm_tiles(M, a.shape[1], N)
    return _matmul(a, b, tm=tm, tn=tn, tk=tk, epilogue=epilogue,
                   out_shapes=[jax.ShapeDtypeStruct((M, N), out_dtype)],
                   out_specs=[pl.BlockSpec((tm, tn), lambda i, j, k: (i, j))], name=name)[0]


def _rope128(x, cos, sin):
    lane = lax.broadcasted_iota(jnp.int32, x.shape, 1)
    half = ROPE_DIM // 2
    partner = jnp.where(lane % ROPE_DIM < half,
                        pltpu.roll(x, LANE - half, 1), pltpu.roll(x, half, 1))
    return x * cos + partner * sin


def _ep_mla_a(acc, extra, outs, *, q_rank, kv_rank):
    qn_ref, kvn_ref, cos_ref, sin_ref = extra
    cq_ref, ckv_ref, kpe_ref = outs
    cq_ref[...] = _rms(acc[:, :q_rank], qn_ref[...]).astype(cq_ref.dtype)
    ckv_ref[...] = _rms(acc[:, q_rank:q_rank + kv_rank], kvn_ref[...]).astype(ckv_ref.dtype)
    kpe = acc[:, q_rank + kv_rank:]
    kpe_ref[...] = _rope128(kpe, cos_ref[...], sin_ref[...]).astype(kpe_ref.dtype)


def _ep_mla_q(acc, extra, outs, *, heads, scale):
    cos_ref, sin_ref = extra
    o_ref = outs[0]
    cos = cos_ref[...]
    sin = sin_ref[...]
    for hh in range(heads):
        base = hh * QK_PAD
        o_ref[:, base:base + NOPE_DIM] = (acc[:, base:base + NOPE_DIM] * scale).astype(o_ref.dtype)
        pe = _rope128(acc[:, base + NOPE_DIM:base + QK_PAD], cos, sin)
        o_ref[:, base + NOPE_DIM:base + QK_PAD] = (pe * scale).astype(o_ref.dtype)


def _kv_kernel(c_ref, wk_ref, wv_ref, kpe_ref, k_ref, v_ref, *, heads):
    c = c_ref[...]
    kn = jnp.dot(c, wk_ref[...], preferred_element_type=_F32)
    vv = jnp.dot(c, wv_ref[...], preferred_element_type=_F32)
    kpe = kpe_ref[...]
    ones = jnp.ones((c.shape[0], V_DIM), v_ref.dtype)
    for hh in range(heads):
        k_ref[:, hh * QK_PAD:hh * QK_PAD + NOPE_DIM] = (
            kn[:, hh * NOPE_DIM:(hh + 1) * NOPE_DIM].astype(k_ref.dtype))
        k_ref[:, hh * QK_PAD + NOPE_DIM:(hh + 1) * QK_PAD] = kpe
        v_ref[:, 2 * hh * V_DIM:(2 * hh + 1) * V_DIM] = (
            vv[:, hh * V_DIM:(hh + 1) * V_DIM].astype(v_ref.dtype))
        v_ref[:, (2 * hh + 1) * V_DIM:(2 * hh + 2) * V_DIM] = ones


def _kv_proj(ckv, wk, wv, kpe, tm, heads_per_step):
    L, R = ckv.shape
    H = wk.shape[1] // NOPE_DIM
    g = heads_per_step
    kern = functools.partial(_kv_kernel, heads=g)
    vmem = 2 * (tm * R * 2 + 2 * R * g * 128 * 2 + tm * 128 * 2 + tm * g * 512 * 2) + tm * g * 256 * 8
    return pl.pallas_call(
        kern,
        grid=(L // tm, H // g),
        in_specs=[pl.BlockSpec((tm, R), lambda i, j: (i, 0)),
                  pl.BlockSpec((R, g * NOPE_DIM), lambda i, j: (0, j)),
                  pl.BlockSpec((R, g * V_DIM), lambda i, j: (0, j)),
                  pl.BlockSpec((tm, LANE), lambda i, j: (i, 0))],
        out_specs=[pl.BlockSpec((tm, g * QK_PAD), lambda i, j: (i, j)),
                   pl.BlockSpec((tm, g * 2 * V_DIM), lambda i, j: (i, j))],
        out_shape=[jax.ShapeDtypeStruct((L, H * QK_PAD), _BF16),
                   jax.ShapeDtypeStruct((L, H * 2 * V_DIM), _BF16)],
        compiler_params=_cparams(("parallel", "parallel"), vmem),
        name="mla_kv",
    )(ckv, wk, wv, kpe)


def _attn_kernel(q_ref, k_ref, v_ref, o_ref, s0_sc, s1_sc, p0_sc, p1_sc, m_sc, acc_sc, *, t):
    qi = pl.program_id(1)
    ncol = t // LANE

    def key_rows(j):
        return pl.ds(pl.multiple_of(j * t, t), t)

    def scores(j, s_ref):
        s_ref[...] = lax.dot_general(q_ref[...], k_ref[key_rows(j), :], _NT,
                                     preferred_element_type=_F32)

    def value_product(j, p_ref):
        return jnp.dot(p_ref[...], v_ref[key_rows(j), :], preferred_element_type=_F32)

    def stage(s_ref, p_ref, nxt, s_nxt, prev, p_prev, masked=False):
        scores(nxt, s_nxt)
        cols = [s_ref[:, c * LANE:(c + 1) * LANE] for c in range(ncol)]
        if masked:
            row = lax.broadcasted_iota(jnp.int32, (t, LANE), 0)
            lane = lax.broadcasted_iota(jnp.int32, (t, LANE), 1)
            cols = [jnp.where(lane + c * LANE <= row, col, -jnp.inf)
                    for c, col in enumerate(cols)]
        mx = cols[0]
        for col in cols[1:]:
            mx = jnp.maximum(mx, col)
        rowmax = jnp.max(mx, axis=-1, keepdims=True)
        if prev is None:
            m_new = jnp.broadcast_to(rowmax, m_sc.shape)
        else:
            m_prev = m_sc[...]
            m_new = jnp.maximum(m_prev, rowmax)
            alpha = jnp.exp2(m_prev - m_new)
            acc = acc_sc[...] + value_product(prev, p_prev)
            acc_sc[...] = acc * jnp.concatenate([alpha, alpha], axis=1)
        for c, col in enumerate(cols):
            p_ref[:, c * LANE:(c + 1) * LANE] = jnp.exp2(col - m_new).astype(p_ref.dtype)
        m_sc[...] = m_new

    acc_sc[...] = jnp.zeros(acc_sc.shape, _F32)
    scores(qi, s0_sc)
    stage(s0_sc, p0_sc, 0, s1_sc, None, None, masked=True)

    def pair(i, carry):
        u = 2 * i
        stage(s1_sc, p1_sc, u + 1, s0_sc, jnp.where(i == 0, qi, u - 1), p0_sc)
        stage(s0_sc, p0_sc, jnp.minimum(u + 2, qi - 1), s1_sc, u, p1_sc)
        return carry

    lax.fori_loop(0, qi // 2, pair, 0)

    @pl.when(qi % 2 == 1)
    def _():
        u = qi - 1
        stage(s1_sc, p1_sc, u, s0_sc, jnp.where(qi == 1, qi, u - 1), p0_sc)
        acc_sc[...] = acc_sc[...] + value_product(u, p1_sc)

    @pl.when(qi % 2 == 0)
    def _():
        acc_sc[...] = acc_sc[...] + value_product(jnp.where(qi == 0, qi, qi - 1), p0_sc)

    acc = acc_sc[...]
    o_ref[...] = (acc[:, :V_DIM] / acc[:, V_DIM:]).astype(o_ref.dtype)


def _attention(q, k, v, t):
    L = q.shape[0]
    H = q.shape[1] // QK_PAD
    kern = functools.partial(_attn_kernel, t=t)
    once = pl.Buffered(1)
    vmem = (L * QK_PAD * 2 + L * 2 * V_DIM * 2 + 2 * (t * QK_PAD * 2 + t * V_DIM * 2)
            + 2 * t * t * 6 + 3 * t * 2 * V_DIM * 4 + t * LANE * 4)
    return pl.pallas_call(
        kern,
        grid=(H, L // t),
        in_specs=[pl.BlockSpec((t, QK_PAD), lambda h, i: (i, h)),
                  pl.BlockSpec((L, QK_PAD), lambda h, i: (0, h), pipeline_mode=once),
                  pl.BlockSpec((L, 2 * V_DIM), lambda h, i: (0, h), pipeline_mode=once)],
        out_specs=pl.BlockSpec((t, V_DIM), lambda h, i: (i, h)),
        out_shape=jax.ShapeDtypeStruct((L, H * V_DIM), _BF16),
        scratch_shapes=[pltpu.VMEM((t, t), _F32), pltpu.VMEM((t, t), _F32),
                        pltpu.VMEM((t, t), _BF16), pltpu.VMEM((t, t), _BF16),
                        pltpu.VMEM((t, LANE), _F32), pltpu.VMEM((t, 2 * V_DIM), _F32)],
        compiler_params=_cparams(("parallel", "arbitrary"), vmem),
        name="mla_attention",
    )(q, k, v)


def _mla(hn, w_a, q_norm, kv_norm, w_qb, w_kb, w_vb, w_o, cos, sin, tiles):
    L, D = hn.shape
    q_rank = q_norm.shape[-1]
    kv_rank = kv_norm.shape[-1]
    heads = w_kb.shape[1] // NOPE_DIM
    tm = _pick(L, (640, 512, 256))
    row_tab = pl.BlockSpec((tm, LANE), lambda i, j, k: (i, 0))
    n_a = w_a.shape[1]
    cq, ckv, kpe = _matmul(
        hn, w_a, tm=tm, tn=n_a, tk=_pick(D, (1024, 512, 256, 128)),
        epilogue=functools.partial(_ep_mla_a, q_rank=q_rank, kv_rank=kv_rank),
        out_shapes=[jax.ShapeDtypeStruct((L, q_rank), _BF16),
                    jax.ShapeDtypeStruct((L, kv_rank), _BF16),
                    jax.ShapeDtypeStruct((L, LANE), _BF16)],
        out_specs=[pl.BlockSpec((tm, q_rank), lambda i, j, k: (i, 0)),
                   pl.BlockSpec((tm, kv_rank), lambda i, j, k: (i, 0)),
                   pl.BlockSpec((tm, LANE), lambda i, j, k: (i, 0))],
        extras=(q_norm.reshape(1, -1), kv_norm.reshape(1, -1), cos, sin),
        extra_specs=(pl.BlockSpec((1, q_rank), lambda i, j, k: (0, 0)),
                     pl.BlockSpec((1, kv_rank), lambda i, j, k: (0, 0)), row_tab, row_tab),
        name="mla_latents")
    g = _pick(heads, (4, 2, 1))
    q = _matmul(
        cq, w_qb, tm=tm, tn=g * QK_PAD, tk=q_rank,
        epilogue=functools.partial(_ep_mla_q, heads=g,
                                   scale=(NOPE_DIM + ROPE_DIM) ** -0.5 * math.log2(math.e)),
        out_shapes=[jax.ShapeDtypeStruct((L, heads * QK_PAD), _BF16)],
        out_specs=[pl.BlockSpec((tm, g * QK_PAD), lambda i, j, k: (i, j))],
        extras=(cos, sin), extra_specs=(row_tab, row_tab), name="mla_q")[0]
    k, v = _kv_proj(ckv, w_kb, w_vb, kpe, tm, g)
    o = _attention(q, k, v, tiles["attn"])
    return _plain_matmul(o, w_o, _F32, name="mla_out")


def _chunk_solve(a_mat, rhs, same_sub, eye):
    p = jnp.where(same_sub, -a_mat, 0.0)
    t = eye + p
    for _ in range(int(math.log2(SUB)) - 1):
        p = _bdot(p, p)
        t = t + _bdot(t, p)
    m = -_bdot(t, jnp.where(same_sub, 0.0, a_mat))
    x = _bdot(t, rhs)
    levels = int(math.log2(CHUNK // SUB))
    for lvl in range(levels):
        x = x + _bdot(m, x)
        if lvl + 1 < levels:
            m = _bdot(m, m)
    return x


def _gdn_kernel(pq_ref, pk_ref, pv_ref, pz_ref, wq_ref, wk_ref, wv_ref, ba_ref, gp_ref, on_ref,
                o_ref, xq_sc, xk_sc, xv_sc, s_sc, *, rows, hb, n_heads):
    grp = pl.program_id(0)
    c = pl.program_id(1)
    tail = GDN_CONV - 1
    nch = rows // CHUNK

    @pl.when(c == 0)
    def _():
        zero8 = jnp.zeros((8, hb * GDN_DIM), _F32)
        xq_sc[0:8, :] = zero8
        xk_sc[0:8, :] = zero8
        xv_sc[0:8, :] = zero8
        s_sc[...] = jnp.zeros(s_sc.shape, _F32)

    def conv_silu(p_ref, w_ref, x_sc):
        x_sc[8:8 + rows, :] = p_ref[...].astype(_F32)
        w = w_ref[...]
        y = w[tail:tail + 1, :] * x_sc[8:8 + rows, :]
        for j in range(tail):
            y = y + w[j:j + 1, :] * x_sc[8 - tail + j:8 - tail + j + rows, :]
        x_sc[0:8, :] = x_sc[rows:rows + 8, :]
        return y * _sigmoid(y)

    def l2norm(t):
        return t * lax.rsqrt(jnp.sum(t * t, axis=-1, keepdims=True) + NORM_EPS)

    q_all = conv_silu(pq_ref, wq_ref, xq_sc)
    k_all = conv_silu(pk_ref, wk_ref, xk_sc)
    v_all = conv_silu(pv_ref, wv_ref, xv_sc)

    ba = ba_ref[...]
    lane = lax.broadcasted_iota(jnp.int32, ba.shape, 1)
    xa = ba + gp_ref[1:2, :]
    softplus = jnp.maximum(xa, 0.0) + jnp.log(1.0 + jnp.exp(-jnp.abs(xa)))
    g_all = -jnp.exp(gp_ref[0:1, :]) * softplus
    sig_all = _sigmoid(ba)

    rr = lax.broadcasted_iota(jnp.int32, (rows, rows), 0)
    cc = lax.broadcasted_iota(jnp.int32, (rows, rows), 1)
    lower = (rr // CHUNK == cc // CHUNK) & (rr >= cc)
    strict = rr > cc
    same_sub = rr // SUB == cc // SUB
    eye = (rr == cc).astype(_F32)
    tri = jnp.where(lower, 1.0, 0.0).astype(_BF16)
    lane0 = lax.broadcasted_iota(jnp.int32, (rows, LANE), 1) == 0
    ones16 = jnp.ones((16, LANE), _BF16)
    onorm = on_ref[...]

    for hh in range(hb):
        head = grp * hb + hh
        hs = slice(hh * GDN_DIM, (hh + 1) * GDN_DIM)
        q = l2norm(q_all[:, hs]) * (GDN_DIM ** -0.5)
        k = l2norm(k_all[:, hs])
        v = v_all[:, hs]
        beta = jnp.sum(jnp.where(lane == head, sig_all, 0.0), axis=-1, keepdims=True)
        g = jnp.sum(jnp.where(lane == n_heads + head, g_all, 0.0), axis=-1, keepdims=True)

        gc = jnp.zeros((rows, LANE), _F32)
        for part in _split3(jnp.broadcast_to(g, (rows, LANE))):
            gc = gc + jnp.dot(tri, part, preferred_element_type=_F32)
        gc_row = jnp.zeros((16, rows), _F32)
        for part in _split3(jnp.where(lane0, gc, 0.0)):
            gc_row = gc_row + lax.dot_general(ones16, part, _NT, preferred_element_type=_F32)

        gc_wide = jnp.concatenate([gc] * (rows // LANE), axis=1)
        dec = jnp.exp(jnp.where(lower, gc_wide - gc_row[0:1, :], -jnp.inf))
        egc = jnp.exp(gc)
        g_last = [gc[(ci + 1) * CHUNK - 1:(ci + 1) * CHUNK, :] for ci in range(nch)]
        g_end = jnp.concatenate([jnp.broadcast_to(gl, (CHUNK, LANE)) for gl in g_last], axis=0)
        kb = k * beta
        a_mat = jnp.where(strict, _bdot(kb, k, _NT) * dec, 0.0)
        attn = _bdot(q, k, _NT) * dec
        x = _chunk_solve(a_mat, jnp.concatenate([kb * egc, v * beta], axis=1), same_sub, eye)
        ax = _bdot(attn, x)
        q_eff = q * egc - ax[:, :GDN_DIM]
        o_intra = ax[:, GDN_DIM:]
        kd = k * jnp.exp(g_end - gc)
        z = pz_ref[:, hs].astype(_F32)
        gate = z * _sigmoid(z)

        s = s_sc[hh]
        for ci in range(nch):
            sl = slice(ci * CHUNK, (ci + 1) * CHUNK)
            kx = _bdot(kd[sl], x[sl], _TN)
            o = _bdot(q_eff[sl], s) + o_intra[sl]
            o_ref[sl, hs] = (_rms(o, onorm) * gate[sl]).astype(o_ref.dtype)
            s = s * jnp.exp(g_last[ci]) - _bdot(kx[:, :GDN_DIM], s) + kx[:, GDN_DIM:]
        s_sc[hh] = s


def _gdn_mixer(proj, ba, conv_w, gate_params, o_norm, rows, hb):
    L = proj.shape[0]
    H = proj.shape[1] // (4 * GDN_DIM)
    kern = functools.partial(_gdn_kernel, rows=rows, hb=hb, n_heads=H)
    width = hb * GDN_DIM
    groups = H // hb

    def col_block(part):
        return pl.BlockSpec((rows, width), lambda g, c: (c, part * groups + g))

    def w_block(part):
        return pl.BlockSpec((GDN_CONV, width), lambda g, c: (0, part * groups + g))

    itemsize = jnp.dtype(proj.dtype).itemsize
    vmem = (2 * (4 * rows * width * itemsize + rows * LANE * 4 + rows * width * 2)
            + 3 * (rows + 8) * width * 4 + (16 << 20))
    return pl.pallas_call(
        kern,
        grid=(groups, L // rows),
        in_specs=[col_block(0), col_block(1), col_block(2), col_block(3),
                  w_block(0), w_block(1), w_block(2),
                  pl.BlockSpec((rows, LANE), lambda g, c: (c, 0)),
                  pl.BlockSpec((8, LANE), lambda g, c: (0, 0)),
                  pl.BlockSpec((1, GDN_DIM), lambda g, c: (0, 0))],
        out_specs=pl.BlockSpec((rows, width), lambda g, c: (c, g)),
        out_shape=jax.ShapeDtypeStruct((L, H * GDN_DIM), _BF16),
        scratch_shapes=[pltpu.VMEM((rows + 8, width), _F32)] * 3
                       + [pltpu.VMEM((hb, GDN_DIM, GDN_DIM), _F32)],
        compiler_params=_cparams(("parallel", "arbitrary"), vmem),
        name="gdn_delta_rule",
    )(proj, proj, proj, proj, conv_w, conv_w, conv_w, ba, gate_params, o_norm.reshape(1, -1))


def _gdn(hn, w_qkvz, w_ba, conv_w, gate_params, o_norm, w_o, tiles):
    heads = w_qkvz.shape[1] // (4 * GDN_DIM)
    proj = _plain_matmul(hn, w_qkvz, _F32, name="gdn_qkvz")
    ba = _plain_matmul(hn, w_ba, _F32, name="gdn_ba")
    o = _gdn_mixer(proj, ba, conv_w, gate_params, o_norm, tiles["gdn_rows"],
                   _pick(heads, (4, 2, 1)))
    return _plain_matmul(o, w_o, _F32, name="gdn_out")


def _mlp(hn, w_up, w_down, tiles):
    up = _plain_matmul(hn, w_up, _BF16, epilogue=_ep_relu2, name="mlp_up")
    return _plain_matmul(up, w_down, _F32, name="mlp_down")


def _rope_tables(length):
    half = ROPE_DIM // 2
    inv = ROPE_THETA ** (-jnp.arange(0, ROPE_DIM, 2, dtype=_F32) / ROPE_DIM)
    ang = jnp.arange(length, dtype=_F32)[:, None] * inv[None, :]
    cos, sin = jnp.cos(ang), jnp.sin(ang)
    zeros = jnp.zeros((length, LANE - ROPE_DIM), _F32)
    return (jnp.concatenate([cos, cos, zeros], axis=1),
            jnp.concatenate([-sin, sin, zeros], axis=1))


def _mla_weights(wq_a, wq_b, wkv_a, wkv_b, wo, kv_rank):
    D = wq_a.shape[0]
    heads = wq_b.shape[1] // (NOPE_DIM + ROPE_DIM)
    w_a = jnp.concatenate(
        [wq_a, wkv_a, jnp.zeros((D, LANE - ROPE_DIM), wq_a.dtype)], axis=1).astype(_BF16)
    qb = wq_b.reshape(-1, heads, NOPE_DIM + ROPE_DIM)
    qb = jnp.pad(qb, ((0, 0), (0, 0), (0, QK_PAD - NOPE_DIM - ROPE_DIM)))
    w_qb = qb.reshape(-1, heads * QK_PAD).astype(_BF16)
    kvb = wkv_b.reshape(kv_rank, heads, NOPE_DIM + V_DIM)
    w_kb = kvb[:, :, :NOPE_DIM].reshape(kv_rank, heads * NOPE_DIM).astype(_BF16)
    w_vb = kvb[:, :, NOPE_DIM:].reshape(kv_rank, heads * V_DIM).astype(_BF16)
    return w_a, w_qb, w_kb, w_vb, wo.astype(_BF16)


def kernel(x, meta_tokens, norm_gains, mla_wq_a, mla_q_norm, mla_wq_b, mla_wkv_a, mla_kv_norm,
           mla_wkv_b, mla_wo, gdn_w_qkvz, gdn_w_ba, gdn_conv_w, gdn_a_log, gdn_dt_bias,
           gdn_o_norm, gdn_wo, mlp_w_up, mlp_w_down):
    batch, seq, D = x.shape
    n_meta = meta_tokens.shape[0]
    depth = norm_gains.shape[0]
    L = n_meta + seq
    Lp = -(-L // _ROW_ALIGN) * _ROW_ALIGN
    gdn_heads = gdn_w_ba.shape[-1] // 2
    assert 2 * gdn_heads <= LANE and gdn_w_qkvz.shape[-1] == 4 * gdn_heads * GDN_DIM
    tiles = {
        "tr": 256,
        "attn": _pick(Lp, (1280, 1024, 768, 512, 256)),
        "gdn_rows": 256,
    }
    cos, sin = _rope_tables(Lp)
    outs = []
    for b in range(batch):
        h = jnp.concatenate([meta_tokens.astype(x.dtype), x[b],
                             jnp.zeros((Lp - L, D), x.dtype)], axis=0)
        hn = _prenorm(h, norm_gains[0, 0], tiles["tr"])
        for i in range(depth):
            gains = norm_gains[i]
            j = i // 2
            if i % 2 == 0:
                kv_rank = mla_kv_norm.shape[-1]
                w = _mla_weights(mla_wq_a[j], mla_wq_b[j], mla_wkv_a[j], mla_wkv_b[j], mla_wo[j],
                                 kv_rank)
                mix = _mla(hn, w[0], mla_q_norm[j], mla_kv_norm[j], w[1], w[2], w[3], w[4],
                           cos, sin, tiles)
            else:
                w_ba = jnp.pad(gdn_w_ba[j], ((0, 0), (0, LANE - 2 * gdn_heads))).astype(_BF16)
                gate_params = jnp.zeros((8, LANE), _F32)
                gate_params = gate_params.at[0, gdn_heads:2 * gdn_heads].set(gdn_a_log[j])
                gate_params = gate_params.at[1, gdn_heads:2 * gdn_heads].set(gdn_dt_bias[j])
                mix = _gdn(hn, gdn_w_qkvz[j].astype(_BF16), w_ba, gdn_conv_w[j], gate_params,
                           gdn_o_norm[j], gdn_wo[j].astype(_BF16), tiles)
            h, hn = _add_norm(h, mix, gains[1], gains[2], tiles["tr"])
            ff = _mlp(hn, mlp_w_up[i].astype(_BF16), mlp_w_down[i].astype(_BF16), tiles)
            g_next = norm_gains[i + 1, 0] if i + 1 < depth else None
            h, hn = _add_norm(h, ff, gains[3], g_next, tiles["tr"])
        outs.append(h[n_meta:L])
    return jnp.stack(outs, axis=0)
```

```python
import functools
import math

import jax
import jax.numpy as jnp
from jax import lax
from jax.experimental import pallas as pl
from jax.experimental.pallas import tpu as pltpu

NORM_EPS = 1e-6
ROPE_THETA = 10000.0
NOPE_DIM = 128
ROPE_DIM = 64
V_DIM = 128
QK_PAD = 256
GDN_DIM = 128
GDN_CONV = 4
CHUNK = 64
SUB = 16
LANE = 128
MXU_N = 256
_ROW_ALIGN = 256
_VMEM_CAP = 56 * 1024 * 1024

_F32 = jnp.float32
_BF16 = jnp.bfloat16
_NT = (((1,), (1,)), ((), ()))
_TN = (((0,), (0,)), ((), ()))


def _pick(n, candidates):
    for c in candidates:
        if n % c == 0:
            return c
    raise ValueError(f"no tile in {candidates} divides {n}")


def _cparams(sem, vmem_bytes):
    limit = int(min(max(vmem_bytes * 5 // 4 + (4 << 20), 32 << 20), _VMEM_CAP))
    return pltpu.CompilerParams(dimension_semantics=sem, vmem_limit_bytes=limit)


def _bdot(a, b, dims=None):
    a = a.astype(_BF16)
    b = b.astype(_BF16)
    if dims is None:
        return jnp.dot(a, b, preferred_element_type=_F32)
    return lax.dot_general(a, b, dims, preferred_element_type=_F32)


def _split3(x):
    x1 = x.astype(_BF16)
    r1 = x - x1.astype(_F32)
    x2 = r1.astype(_BF16)
    r2 = r1 - x2.astype(_F32)
    x3 = r2.astype(_BF16)
    return x1, x2, x3


def _sigmoid(x):
    return 1.0 / (1.0 + jnp.exp(-x))


def _rms(x, w):
    return x * lax.rsqrt(jnp.mean(x * x, axis=-1, keepdims=True) + NORM_EPS) * w


def _prenorm_kernel(h_ref, g_ref, o_ref):
    o_ref[...] = _rms(h_ref[...], g_ref[...]).astype(o_ref.dtype)


def _prenorm(h, gain, tr):
    L, D = h.shape
    return pl.pallas_call(
        _prenorm_kernel,
        grid=(L // tr,),
        in_specs=[pl.BlockSpec((tr, D), lambda i: (i, 0)),
                  pl.BlockSpec((1, D), lambda i: (0, 0))],
        out_specs=pl.BlockSpec((tr, D), lambda i: (i, 0)),
        out_shape=jax.ShapeDtypeStruct((L, D), _BF16),
        compiler_params=_cparams(("parallel",), 2 * tr * D * 6),
        name="prenorm",
    )(h, gain.reshape(1, D))


def _add_norm_kernel(h_ref, m_ref, gp_ref, gn_ref, ho_ref, hn_ref):
    h = h_ref[...] + _rms(m_ref[...], gp_ref[...])
    ho_ref[...] = h
    hn_ref[...] = _rms(h, gn_ref[...]).astype(hn_ref.dtype)


def _add_kernel(h_ref, m_ref, gp_ref, ho_ref):
    ho_ref[...] = h_ref[...] + _rms(m_ref[...], gp_ref[...])


def _add_norm(h, mix, g_post, g_next, tr):
    L, D = h.shape
    row = pl.BlockSpec((tr, D), lambda i: (i, 0))
    vec = pl.BlockSpec((1, D), lambda i: (0, 0))
    if g_next is None:
        return pl.pallas_call(
            _add_kernel, grid=(L // tr,),
            in_specs=[row, row, vec], out_specs=row,
            out_shape=jax.ShapeDtypeStruct((L, D), _F32),
            compiler_params=_cparams(("parallel",), 2 * tr * D * 12),
            name="add_post",
        )(h, mix, g_post.reshape(1, D)), None
    return pl.pallas_call(
        _add_norm_kernel, grid=(L // tr,),
        in_specs=[row, row, vec, vec], out_specs=[row, row],
        out_shape=[jax.ShapeDtypeStruct((L, D), _F32), jax.ShapeDtypeStruct((L, D), _BF16)],
        compiler_params=_cparams(("parallel",), 2 * tr * D * 14),
        name="add_norm",
    )(h, mix, g_post.reshape(1, D), g_next.reshape(1, D))


def _mm_kernel(*refs, nk, n_extra, n_out, epilogue):
    a_ref, b_ref = refs[0], refs[1]
    extra = refs[2:2 + n_extra]
    outs = refs[2 + n_extra:2 + n_extra + n_out]
    if nk == 1:
        epilogue(jnp.dot(a_ref[...], b_ref[...], preferred_element_type=_F32), extra, outs)
        return
    acc_ref = refs[-1]
    k = pl.program_id(2)

    @pl.when(k == 0)
    def _():
        acc_ref[...] = jnp.zeros_like(acc_ref)

    acc_ref[...] += jnp.dot(a_ref[...], b_ref[...], preferred_element_type=_F32)

    @pl.when(k == nk - 1)
    def _():
        epilogue(acc_ref[...], extra, outs)


def _matmul(a, b, *, tm, tn, tk, epilogue, out_shapes, out_specs, extras=(), extra_specs=(),
            name="matmul"):
    M, K = a.shape
    N = b.shape[1]
    nk = K // tk
    kern = functools.partial(_mm_kernel, nk=nk, n_extra=len(extras), n_out=len(out_shapes),
                             epilogue=epilogue)
    out_bytes = sum(math.prod(s.block_shape) * jnp.dtype(o.dtype).itemsize
                    for s, o in zip(out_specs, out_shapes))
    vmem = 2 * (tm * tk * 2 + tk * tn * 2 + out_bytes) + 2 * tm * tn * 4
    vmem += 2 * sum(math.prod(s.block_shape) * 4 for s in extra_specs)
    return pl.pallas_call(
        kern,
        grid=(M // tm, N // tn, nk),
        in_specs=[pl.BlockSpec((tm, tk), lambda i, j, k: (i, k)),
                  pl.BlockSpec((tk, tn), lambda i, j, k: (k, j))] + list(extra_specs),
        out_specs=list(out_specs),
        out_shape=list(out_shapes),
        scratch_shapes=[pltpu.VMEM((tm, tn), _F32)] if nk > 1 else [],
        compiler_params=_cparams(("parallel", "parallel", "arbitrary"), vmem),
        name=name,
    )(a, b, *extras)


def _ep_store(acc, extra, outs):
    outs[0][...] = acc.astype(outs[0].dtype)


def _ep_relu2(acc, extra, outs):
    r = jnp.maximum(acc, 0.0)
    outs[0][...] = (r * r).astype(outs[0].dtype)


_FULL_K_MAX = 4096


def _mm_tiles(M, K, N):
    tn = _pick(N, (1024, 512, 256, 128))
    if K <= _FULL_K_MAX:
        return _pick(M, (640, 512, 256)), tn, K
    return _pick(M, (1280, 1024, 512, 256)), tn, _pick(K, (2048, 1024, 512, 256, 128))


def _plain_matmul(a, b, out_dtype, *, epilogue=_ep_store, name="matmul"):
    M, N = a.shape[0], b.shape[1]
    tm, tn, tk = _mm_tiles(M, a.shape[1], N)
    return _matmul(a, b, tm=tm, tn=tn, tk=tk, epilogue=epilogue,
                   out_shapes=[jax.ShapeDtypeStruct((M, N), out_dtype)],
                   out_specs=[pl.BlockSpec((tm, tn), lambda i, j, k: (i, j))], name=name)[0]


def _rope128(x, cos, sin):
    lane = lax.broadcasted_iota(jnp.int32, x.shape, 1)
    half = ROPE_DIM // 2
    partner = jnp.where(lane % ROPE_DIM < half,
                        pltpu.roll(x, LANE - half, 1), pltpu.roll(x, half, 1))
    return x * cos + partner * sin


def _ep_mla_a(acc, extra, outs, *, q_rank, kv_rank):
    qn_ref, kvn_ref, cos_ref, sin_ref = extra
    cq_ref, ckv_ref, kpe_ref = outs
    cq_ref[...] = _rms(acc[:, :q_rank], qn_ref[...]).astype(cq_ref.dtype)
    ckv_ref[...] = _rms(acc[:, q_rank:q_rank + kv_rank], kvn_ref[...]).astype(ckv_ref.dtype)
    kpe = acc[:, q_rank + kv_rank:]
    kpe_ref[...] = _rope128(kpe, cos_ref[...], sin_ref[...]).astype(kpe_ref.dtype)


def _ep_mla_q(acc, extra, outs, *, heads, scale):
    cos_ref, sin_ref = extra
    o_ref = outs[0]
    cos = cos_ref[...]
    sin = sin_ref[...]
    for hh in range(heads):
        base = hh * QK_PAD
        o_ref[:, base:base + NOPE_DIM] = (acc[:, base:base + NOPE_DIM] * scale).astype(o_ref.dtype)
        pe = _rope128(acc[:, base + NOPE_DIM:base + QK_PAD], cos, sin)
        o_ref[:, base + NOPE_DIM:base + QK_PAD] = (pe * scale).astype(o_ref.dtype)


def _kv_kernel(c_ref, wk_ref, wv_ref, kpe_ref, k_ref, v_ref, *, heads):
    c = c_ref[...]
    kn = jnp.dot(c, wk_ref[...], preferred_element_type=_F32)
    vv = jnp.dot(c, wv_ref[...], preferred_element_type=_F32)
    kpe = kpe_ref[...]
    ones = jnp.ones((c.shape[0], V_DIM), v_ref.dtype)
    for hh in range(heads):
        k_ref[:, hh * QK_PAD:hh * QK_PAD + NOPE_DIM] = (
            kn[:, hh * NOPE_DIM:(hh + 1) * NOPE_DIM].astype(k_ref.dtype))
        k_ref[:, hh * QK_PAD + NOPE_DIM:(hh + 1) * QK_PAD] = kpe
        v_ref[:, 2 * hh * V_DIM:(2 * hh + 1) * V_DIM] = (
            vv[:, hh * V_DIM:(hh + 1) * V_DIM].astype(v_ref.dtype))
        v_ref[:, (2 * hh + 1) * V_DIM:(2 * hh + 2) * V_DIM] = ones


def _kv_proj(ckv, wk, wv, kpe, tm, heads_per_step):
    L, R = ckv.shape
    H = wk.shape[1] // NOPE_DIM
    g = heads_per_step
    kern = functools.partial(_kv_kernel, heads=g)
    vmem = 2 * (tm * R * 2 + 2 * R * g * 128 * 2 + tm * 128 * 2 + tm * g * 512 * 2) + tm * g * 256 * 8
    return pl.pallas_call(
        kern,
        grid=(L // tm, H // g),
        in_specs=[pl.BlockSpec((tm, R), lambda i, j: (i, 0)),
                  pl.BlockSpec((R, g * NOPE_DIM), lambda i, j: (0, j)),
                  pl.BlockSpec((R, g * V_DIM), lambda i, j: (0, j)),
                  pl.BlockSpec((tm, LANE), lambda i, j: (i, 0))],
        out_specs=[pl.BlockSpec((tm, g * QK_PAD), lambda i, j: (i, j)),
                   pl.BlockSpec((tm, g * 2 * V_DIM), lambda i, j: (i, j))],
        out_shape=[jax.ShapeDtypeStruct((L, H * QK_PAD), _BF16),
                   jax.ShapeDtypeStruct((L, H * 2 * V_DIM), _BF16)],
        compiler_params=_cparams(("parallel", "parallel"), vmem),
        name="mla_kv",
    )(ckv, wk, wv, kpe)


def _attn_kernel(q_ref, k_ref, v_ref, o_ref, s0_sc, s1_sc, p0_sc, p1_sc, m_sc, acc_sc, *, t):
    qi = pl.program_id(1)
    ncol = t // LANE

    def key_rows(j):
        return pl.ds(pl.multiple_of(j * t, t), t)

    def scores(j, s_ref):
        s_ref[...] = lax.dot_general(q_ref[...], k_ref[key_rows(j), :], _NT,
                                     preferred_element_type=_F32)

    def value_product(j, p_ref):
        return jnp.dot(p_ref[...], v_ref[key_rows(j), :], preferred_element_type=_F32)

    def stage(s_ref, p_ref, nxt, s_nxt, prev, p_prev, masked=False):
        scores(nxt, s_nxt)
        cols = [s_ref[:, c * LANE:(c + 1) * LANE] for c in range(ncol)]
        if masked:
            row = lax.broadcasted_iota(jnp.int32, (t, LANE), 0)
            lane = lax.broadcasted_iota(jnp.int32, (t, LANE), 1)
            cols = [jnp.where(lane + c * LANE <= row, col, -jnp.inf)
                    for c, col in enumerate(cols)]
        mx = cols[0]
        for col in cols[1:]:
            mx = jnp.maximum(mx, col)
        rowmax = jnp.max(mx, axis=-1, keepdims=True)
        if prev is None:
            m_new = jnp.broadcast_to(rowmax, m_sc.shape)
        else:
            m_prev = m_sc[...]
            m_new = jnp.maximum(m_prev, rowmax)
            alpha = jnp.exp2(m_prev - m_new)
            acc = acc_sc[...] + value_product(prev, p_prev)
            acc_sc[...] = acc * jnp.concatenate([alpha, alpha], axis=1)
        for c, col in enumerate(cols):
            p_ref[:, c * LANE:(c + 1) * LANE] = jnp.exp2(col - m_new).astype(p_ref.dtype)
        m_sc[...] = m_new

    acc_sc[...] = jnp.zeros(acc_sc.shape, _F32)
    for c0 in range(0, t, MXU_N):
        keys = pl.ds(pl.multiple_of(qi * t + c0, MXU_N), MXU_N)
        s0_sc[c0:, c0:c0 + MXU_N] = lax.dot_general(q_ref[c0:, :], k_ref[keys, :], _NT,
                                                     preferred_element_type=_F32)
        if c0:
            s0_sc[:c0, c0:c0 + MXU_N] = jnp.full((c0, MXU_N), -jnp.inf, _F32)
    stage(s0_sc, p0_sc, 0, s1_sc, None, None, masked=True)

    def pair(i, carry):
        u = 2 * i
        stage(s1_sc, p1_sc, u + 1, s0_sc, jnp.where(i == 0, qi, u - 1), p0_sc)
        stage(s0_sc, p0_sc, jnp.minimum(u + 2, qi - 1), s1_sc, u, p1_sc)
        return carry

    lax.fori_loop(0, qi // 2, pair, 0)

    @pl.when(qi % 2 == 1)
    def _():
        u = qi - 1
        stage(s1_sc, p1_sc, u, s0_sc, jnp.where(qi == 1, qi, u - 1), p0_sc)
        acc_sc[...] = acc_sc[...] + value_product(u, p1_sc)

    @pl.when(qi % 2 == 0)
    def _():
        acc_sc[...] = acc_sc[...] + value_product(jnp.where(qi == 0, qi, qi - 1), p0_sc)

    acc = acc_sc[...]
    o_ref[...] = (acc[:, :V_DIM] / acc[:, V_DIM:]).astype(o_ref.dtype)


def _attention(q, k, v, t):
    L = q.shape[0]
    H = q.shape[1] // QK_PAD
    kern = functools.partial(_attn_kernel, t=t)
    once = pl.Buffered(1)
    vmem = (L * QK_PAD * 2 + L * 2 * V_DIM * 2 + 2 * (t * QK_PAD * 2 + t * V_DIM * 2)
            + 2 * t * t * 6 + 3 * t * 2 * V_DIM * 4 + t * LANE * 4)
    return pl.pallas_call(
        kern,
        grid=(H, L // t),
        in_specs=[pl.BlockSpec((t, QK_PAD), lambda h, i: (i, h)),
                  pl.BlockSpec((L, QK_PAD), lambda h, i: (0, h), pipeline_mode=once),
                  pl.BlockSpec((L, 2 * V_DIM), lambda h, i: (0, h), pipeline_mode=once)],
        out_specs=pl.BlockSpec((t, V_DIM), lambda h, i: (i, h)),
        out_shape=jax.ShapeDtypeStruct((L, H * V_DIM), _BF16),
        scratch_shapes=[pltpu.VMEM((t, t), _F32), pltpu.VMEM((t, t), _F32),
                        pltpu.VMEM((t, t), _BF16), pltpu.VMEM((t, t), _BF16),
                        pltpu.VMEM((t, LANE), _F32), pltpu.VMEM((t, 2 * V_DIM), _F32)],
        compiler_params=_cparams(("parallel", "arbitrary"), vmem),
        name="mla_attention",
    )(q, k, v)


def _mla(hn, w_a, q_norm, kv_norm, w_qb, w_kb, w_vb, w_o, cos, sin, tiles):
    L, D = hn.shape
    q_rank = q_norm.shape[-1]
    kv_rank = kv_norm.shape[-1]
    heads = w_kb.shape[1] // NOPE_DIM
    tm = _pick(L, (640, 512, 256))
    row_tab = pl.BlockSpec((tm, LANE), lambda i, j, k: (i, 0))
    n_a = w_a.shape[1]
    cq, ckv, kpe = _matmul(
        hn, w_a, tm=tm, tn=n_a, tk=_pick(D, (1024, 512, 256, 128)),
        epilogue=functools.partial(_ep_mla_a, q_rank=q_rank, kv_rank=kv_rank),
        out_shapes=[jax.ShapeDtypeStruct((L, q_rank), _BF16),
                    jax.ShapeDtypeStruct((L, kv_rank), _BF16),
                    jax.ShapeDtypeStruct((L, LANE), _BF16)],
        out_specs=[pl.BlockSpec((tm, q_rank), lambda i, j, k: (i, 0)),
                   pl.BlockSpec((tm, kv_rank), lambda i, j, k: (i, 0)),
                   pl.BlockSpec((tm, LANE), lambda i, j, k: (i, 0))],
        extras=(q_norm.reshape(1, -1), kv_norm.reshape(1, -1), cos, sin),
        extra_specs=(pl.BlockSpec((1, q_rank), lambda i, j, k: (0, 0)),
                     pl.BlockSpec((1, kv_rank), lambda i, j, k: (0, 0)), row_tab, row_tab),
        name="mla_latents")
    g = _pick(heads, (4, 2, 1))
    q = _matmul(
        cq, w_qb, tm=tm, tn=g * QK_PAD, tk=q_rank,
        epilogue=functools.partial(_ep_mla_q, heads=g,
                                   scale=(NOPE_DIM + ROPE_DIM) ** -0.5 * math.log2(math.e)),
        out_shapes=[jax.ShapeDtypeStruct((L, heads * QK_PAD), _BF16)],
        out_specs=[pl.BlockSpec((tm, g * QK_PAD), lambda i, j, k: (i, j))],
        extras=(cos, sin), extra_specs=(row_tab, row_tab), name="mla_q")[0]
    k, v = _kv_proj(ckv, w_kb, w_vb, kpe, tm, g)
    o = _attention(q, k, v, tiles["attn"])
    return _plain_matmul(o, w_o, _F32, name="mla_out")


def _gdn_kernel(pq_ref, pk_ref, pv_ref, pz_ref, wq_ref, wk_ref, wv_ref, ba_ref, gp_ref, on_ref,
                o_ref, xq_sc, xk_sc, xv_sc, s_sc, gt_sc, *, rows, hb, n_heads):
    grp = pl.program_id(0)
    c = pl.program_id(1)
    tail = GDN_CONV - 1
    nch = rows // CHUNK

    @pl.when(c == 0)
    def _():
        zero8 = jnp.zeros((8, hb * GDN_DIM), _F32)
        xq_sc[0:8, :] = zero8
        xk_sc[0:8, :] = zero8
        xv_sc[0:8, :] = zero8
        s_sc[...] = jnp.zeros(s_sc.shape, _F32)

    def conv_silu(p_ref, w_ref, x_sc):
        x_sc[8:8 + rows, :] = p_ref[...].astype(_F32)
        w = w_ref[...]
        y = w[tail:tail + 1, :] * x_sc[8:8 + rows, :]
        for j in range(tail):
            y = y + w[j:j + 1, :] * x_sc[8 - tail + j:8 - tail + j + rows, :]
        x_sc[0:8, :] = x_sc[rows:rows + 8, :]
        return y * _sigmoid(y)

    def l2norm(t):
        return t * lax.rsqrt(jnp.sum(t * t, axis=-1, keepdims=True) + NORM_EPS)

    q_all = conv_silu(pq_ref, wq_ref, xq_sc)
    k_all = conv_silu(pk_ref, wk_ref, xk_sc)
    v_all = conv_silu(pv_ref, wv_ref, xv_sc)

    ba = ba_ref[...]
    lane = lax.broadcasted_iota(jnp.int32, ba.shape, 1)
    xa = ba + gp_ref[1:2, :]
    softplus = jnp.maximum(xa, 0.0) + jnp.log(1.0 + jnp.exp(-jnp.abs(xa)))
    g_all = -jnp.exp(gp_ref[0:1, :]) * softplus
    sig_all = _sigmoid(ba)

    rr = lax.broadcasted_iota(jnp.int32, (rows, rows), 0)
    cc = lax.broadcasted_iota(jnp.int32, (rows, rows), 1)
    lower = (rr // CHUNK == cc // CHUNK) & (rr >= cc)
    strict = rr > cc
    same_sub = rr // SUB == cc // SUB
    eye = (rr == cc).astype(_F32)
    tri = jnp.where(lower, 1.0, 0.0).astype(_BF16)
    onorm = on_ref[...]

    eye_lane = (lax.broadcasted_iota(jnp.int32, (LANE, LANE), 0)
                == lax.broadcasted_iota(jnp.int32, (LANE, LANE), 1)).astype(_BF16)
    gc_all = jnp.zeros((rows, LANE), _F32)
    for part in _split3(g_all):
        gc_all = gc_all + jnp.dot(tri, part, preferred_element_type=_F32)
    gc_t = jnp.zeros((LANE, rows), _F32)
    for part in _split3(gc_all):
        gc_t = gc_t + lax.dot_general(eye_lane, part, _NT, preferred_element_type=_F32)
    gt_sc[...] = gc_t

    hbs = range(hb)
    heads = [grp * hb + hh for hh in hbs]
    hs = [slice(hh * GDN_DIM, (hh + 1) * GDN_DIM) for hh in hbs]
    q = [l2norm(q_all[:, hs[h]]) * (GDN_DIM ** -0.5) for h in hbs]
    k = [l2norm(k_all[:, hs[h]]) for h in hbs]
    v = [v_all[:, hs[h]] for h in hbs]
    beta = [jnp.sum(jnp.where(lane == heads[h], sig_all, 0.0), axis=-1, keepdims=True)
            for h in hbs]
    gc = [jnp.broadcast_to(
        jnp.sum(jnp.where(lane == n_heads + heads[h], gc_all, 0.0), axis=-1, keepdims=True),
        (rows, LANE)) for h in hbs]
    gc_row = [gt_sc[pl.ds(n_heads + heads[h], 1), :] for h in hbs]
    dec = [jnp.exp(jnp.where(lower, jnp.concatenate([gc[h]] * (rows // LANE), axis=1) - gc_row[h],
                             -jnp.inf)) for h in hbs]
    egc = [jnp.exp(gc[h]) for h in hbs]
    g_last = [[gc[h][(ci + 1) * CHUNK - 1:(ci + 1) * CHUNK, :] for ci in range(nch)]
              for h in hbs]
    g_end = [jnp.concatenate([jnp.broadcast_to(gl, (CHUNK, LANE)) for gl in g_last[h]], axis=0)
             for h in hbs]
    kb = [k[h] * beta[h] for h in hbs]
    a_mat = [jnp.where(strict, _bdot(kb[h], k[h], _NT) * dec[h], 0.0) for h in hbs]
    attn = [_bdot(q[h], k[h], _NT) * dec[h] for h in hbs]
    rhs = [jnp.concatenate([kb[h] * egc[h], v[h] * beta[h]], axis=1) for h in hbs]

    p = [jnp.where(same_sub, -a_mat[h], 0.0) for h in hbs]
    t = [eye + p[h] for h in hbs]
    for _ in range(int(math.log2(SUB)) - 1):
        p = [_bdot(p[h], p[h]) for h in hbs]
        t = [t[h] + _bdot(t[h], p[h]) for h in hbs]
    m = [-_bdot(t[h], jnp.where(same_sub, 0.0, a_mat[h])) for h in hbs]
    x = [_bdot(t[h], rhs[h]) for h in hbs]
    levels = int(math.log2(CHUNK // SUB))
    for lvl in range(levels):
        x = [x[h] + _bdot(m[h], x[h]) for h in hbs]
        if lvl + 1 < levels:
            m = [_bdot(m[h], m[h]) for h in hbs]

    ax = [_bdot(attn[h], x[h]) for h in hbs]
    q_eff = [q[h] * egc[h] - ax[h][:, :GDN_DIM] for h in hbs]
    o_intra = [ax[h][:, GDN_DIM:] for h in hbs]
    kd = [k[h] * jnp.exp(g_end[h] - gc[h]) for h in hbs]
    gate = []
    for h in hbs:
        z = pz_ref[:, hs[h]].astype(_F32)
        gate.append(z * _sigmoid(z))

    s = [s_sc[h] for h in hbs]
    for ci in range(nch):
        sl = slice(ci * CHUNK, (ci + 1) * CHUNK)
        kx = [_bdot(kd[h][sl], x[h][sl], _TN) for h in hbs]
        o = [_bdot(q_eff[h][sl], s[h]) + o_intra[h][sl] for h in hbs]
        for h in hbs:
            o_ref[sl, hs[h]] = (_rms(o[h], onorm) * gate[h][sl]).astype(o_ref.dtype)
        s = [s[h] * jnp.exp(g_last[h][ci]) - _bdot(kx[h][:, :GDN_DIM], s[h]) + kx[h][:, GDN_DIM:]
             for h in hbs]
    for h in hbs:
        s_sc[h] = s[h]


def _gdn_mixer(proj, ba, conv_w, gate_params, o_norm, rows, hb):
    L = proj.shape[0]
    H = proj.shape[1] // (4 * GDN_DIM)
    kern = functools.partial(_gdn_kernel, rows=rows, hb=hb, n_heads=H)
    width = hb * GDN_DIM
    groups = H // hb

    def col_block(part):
        return pl.BlockSpec((rows, width), lambda g, c: (c, part * groups + g))

    def w_block(part):
        return pl.BlockSpec((GDN_CONV, width), lambda g, c: (0, part * groups + g))

    itemsize = jnp.dtype(proj.dtype).itemsize
    vmem = (2 * (4 * rows * width * itemsize + rows * LANE * 4 + rows * width * 2)
            + 3 * (rows + 8) * width * 4 + (16 << 20))
    return pl.pallas_call(
        kern,
        grid=(groups, L // rows),
        in_specs=[col_block(0), col_block(1), col_block(2), col_block(3),
                  w_block(0), w_block(1), w_block(2),
                  pl.BlockSpec((rows, LANE), lambda g, c: (c, 0)),
                  pl.BlockSpec((8, LANE), lambda g, c: (0, 0)),
                  pl.BlockSpec((1, GDN_DIM), lambda g, c: (0, 0))],
        out_specs=pl.BlockSpec((rows, width), lambda g, c: (c, g)),
        out_shape=jax.ShapeDtypeStruct((L, H * GDN_DIM), _BF16),
        scratch_shapes=[pltpu.VMEM((rows + 8, width), _F32)] * 3
                       + [pltpu.VMEM((hb, GDN_DIM, GDN_DIM), _F32), pltpu.VMEM((LANE, rows), _F32)],
        compiler_params=_cparams(("parallel", "arbitrary"), vmem),
        name="gdn_delta_rule",
    )(proj, proj, proj, proj, conv_w, conv_w, conv_w, ba, gate_params, o_norm.reshape(1, -1))


def _gdn(hn, w_qkvz, w_ba, conv_w, gate_params, o_norm, w_o, tiles):
    heads = w_qkvz.shape[1] // (4 * GDN_DIM)
    proj = _plain_matmul(hn, w_qkvz, _F32, name="gdn_qkvz")
    ba = _plain_matmul(hn, w_ba, _F32, name="gdn_ba")
    o = _gdn_mixer(proj, ba, conv_w, gate_params, o_norm, tiles["gdn_rows"],
                   _pick(heads, (8, 4, 2, 1)))
    return _plain_matmul(o, w_o, _F32, name="gdn_out")


def _mlp(hn, w_up, w_down, tiles):
    up = _plain_matmul(hn, w_up, _BF16, epilogue=_ep_relu2, name="mlp_up")
    return _plain_matmul(up, w_down, _F32, name="mlp_down")


def _rope_tables(length):
    half = ROPE_DIM // 2
    inv = ROPE_THETA ** (-jnp.arange(0, ROPE_DIM, 2, dtype=_F32) / ROPE_DIM)
    ang = jnp.arange(length, dtype=_F32)[:, None] * inv[None, :]
    cos, sin = jnp.cos(ang), jnp.sin(ang)
    zeros = jnp.zeros((length, LANE - ROPE_DIM), _F32)
    return (jnp.concatenate([cos, cos, zeros], axis=1),
            jnp.concatenate([-sin, sin, zeros], axis=1))


def _mla_weights(wq_a, wq_b, wkv_a, wkv_b, wo, kv_rank):
    D = wq_a.shape[0]
    heads = wq_b.shape[1] // (NOPE_DIM + ROPE_DIM)
    w_a = jnp.concatenate(
        [wq_a, wkv_a, jnp.zeros((D, LANE - ROPE_DIM), wq_a.dtype)], axis=1).astype(_BF16)
    qb = wq_b.reshape(-1, heads, NOPE_DIM + ROPE_DIM)
    qb = jnp.pad(qb, ((0, 0), (0, 0), (0, QK_PAD - NOPE_DIM - ROPE_DIM)))
    w_qb = qb.reshape(-1, heads * QK_PAD).astype(_BF16)
    kvb = wkv_b.reshape(kv_rank, heads, NOPE_DIM + V_DIM)
    w_kb = kvb[:, :, :NOPE_DIM].reshape(kv_rank, heads * NOPE_DIM).astype(_BF16)
    w_vb = kvb[:, :, NOPE_DIM:].reshape(kv_rank, heads * V_DIM).astype(_BF16)
    return w_a, w_qb, w_kb, w_vb, wo.astype(_BF16)


def kernel(x, meta_tokens, norm_gains, mla_wq_a, mla_q_norm, mla_wq_b, mla_wkv_a, mla_kv_norm,
           mla_wkv_b, mla_wo, gdn_w_qkvz, gdn_w_ba, gdn_conv_w, gdn_a_log, gdn_dt_bias,
           gdn_o_norm, gdn_wo, mlp_w_up, mlp_w_down):
    batch, seq, D = x.shape
    n_meta = meta_tokens.shape[0]
    depth = norm_gains.shape[0]
    L = n_meta + seq
    Lp = -(-L // _ROW_ALIGN) * _ROW_ALIGN
    gdn_heads = gdn_w_ba.shape[-1] // 2
    assert 2 * gdn_heads <= LANE and gdn_w_qkvz.shape[-1] == 4 * gdn_heads * GDN_DIM
    tiles = {
        "tr": 256,
        "attn": _pick(Lp, (1280, 1024, 768, 512, 256)),
        "gdn_rows": 256,
    }
    cos, sin = _rope_tables(Lp)
    outs = []
    for b in range(batch):
        h = jnp.concatenate([meta_tokens.astype(x.dtype), x[b],
                             jnp.zeros((Lp - L, D), x.dtype)], axis=0)
        hn = _prenorm(h, norm_gains[0, 0], tiles["tr"])
        for i in range(depth):
            gains = norm_gains[i]
            j = i // 2
            if i % 2 == 0:
                kv_rank = mla_kv_norm.shape[-1]
                w = _mla_weights(mla_wq_a[j], mla_wq_b[j], mla_wkv_a[j], mla_wkv_b[j], mla_wo[j],
                                 kv_rank)
                mix = _mla(hn, w[0], mla_q_norm[j], mla_kv_norm[j], w[1], w[2], w[3], w[4],
                           cos, sin, tiles)
            else:
                w_ba = jnp.pad(gdn_w_ba[j], ((0, 0), (0, LANE - 2 * gdn_heads))).astype(_BF16)
                gate_params = jnp.zeros((8, LANE), _F32)
                gate_params = gate_params.at[0, gdn_heads:2 * gdn_heads].set(gdn_a_log[j])
                gate_params = gate_params.at[1, gdn_heads:2 * gdn_heads].set(gdn_dt_bias[j])
                mix = _gdn(hn, gdn_w_qkvz[j].astype(_BF16), w_ba, gdn_conv_w[j], gate_params,
                           gdn_o_norm[j], gdn_wo[j].astype(_BF16), tiles)
            h, hn = _add_norm(h, mix, gains[1], gains[2], tiles["tr"])
            ff = _mlp(hn, mlp_w_up[i].astype(_BF16), mlp_w_down[i].astype(_BF16), tiles)
            g_next = norm_gains[i + 1, 0] if i + 1 < depth else None
            h, hn = _add_norm(h, ff, gains[3], g_next, tiles["tr"])
        outs.append(h[n_meta:L])
    return jnp.stack(outs, axis=0)
```

```python
import functools
import math

import jax
import jax.numpy as jnp
from jax import lax
from jax.experimental import pallas as pl
from jax.experimental.pallas import tpu as pltpu

NORM_EPS = 1e-6
ROPE_THETA = 10000.0
NOPE_DIM = 128
ROPE_DIM = 64
V_DIM = 128
QK_PAD = 256
GDN_DIM = 128
GDN_CONV = 4
CHUNK = 64
SUB = 16
LANE = 128
MXU_N = 256
_ROW_ALIGN = 256
_VMEM_CAP = 56 * 1024 * 1024

_F32 = jnp.float32
_BF16 = jnp.bfloat16
_NT = (((1,), (1,)), ((), ()))
_TN = (((0,), (0,)), ((), ()))


def _pick(n, candidates):
    for c in candidates:
        if n % c == 0:
            return c
    raise ValueError(f"no tile in {candidates} divides {n}")


def _cparams(sem, vmem_bytes):
    limit = int(min(max(vmem_bytes * 5 // 4 + (4 << 20), 32 << 20), _VMEM_CAP))
    return pltpu.CompilerParams(dimension_semantics=sem, vmem_limit_bytes=limit)


def _bdot(a, b, dims=None):
    a = a.astype(_BF16)
    b = b.astype(_BF16)
    if dims is None:
        return jnp.dot(a, b, preferred_element_type=_F32)
    return lax.dot_general(a, b, dims, preferred_element_type=_F32)


def _split3(x):
    x1 = x.astype(_BF16)
    r1 = x - x1.astype(_F32)
    x2 = r1.astype(_BF16)
    r2 = r1 - x2.astype(_F32)
    x3 = r2.astype(_BF16)
    return x1, x2, x3


def _sigmoid(x):
    return 1.0 / (1.0 + jnp.exp(-x))


def _rms(x, w):
    return x * lax.rsqrt(jnp.mean(x * x, axis=-1, keepdims=True) + NORM_EPS) * w


def _prenorm_kernel(h_ref, g_ref, o_ref):
    o_ref[...] = _rms(h_ref[...], g_ref[...]).astype(o_ref.dtype)


def _prenorm(h, gain, tr):
    L, D = h.shape
    return pl.pallas_call(
        _prenorm_kernel,
        grid=(L // tr,),
        in_specs=[pl.BlockSpec((tr, D), lambda i: (i, 0)),
                  pl.BlockSpec((1, D), lambda i: (0, 0))],
        out_specs=pl.BlockSpec((tr, D), lambda i: (i, 0)),
        out_shape=jax.ShapeDtypeStruct((L, D), _BF16),
        compiler_params=_cparams(("parallel",), 2 * tr * D * 6),
        name="prenorm",
    )(h, gain.reshape(1, D))


def _add_norm_kernel(h_ref, m_ref, gp_ref, gn_ref, ho_ref, hn_ref):
    h = h_ref[...] + _rms(m_ref[...], gp_ref[...])
    ho_ref[...] = h
    hn_ref[...] = _rms(h, gn_ref[...]).astype(hn_ref.dtype)


def _add_kernel(h_ref, m_ref, gp_ref, ho_ref):
    ho_ref[...] = h_ref[...] + _rms(m_ref[...], gp_ref[...])


def _add_norm(h, mix, g_post, g_next, tr):
    L, D = h.shape
    row = pl.BlockSpec((tr, D), lambda i: (i, 0))
    vec = pl.BlockSpec((1, D), lambda i: (0, 0))
    if g_next is None:
        return pl.pallas_call(
            _add_kernel, grid=(L // tr,),
            in_specs=[row, row, vec], out_specs=row,
            out_shape=jax.ShapeDtypeStruct((L, D), _F32),
            compiler_params=_cparams(("parallel",), 2 * tr * D * 12),
            name="add_post",
        )(h, mix, g_post.reshape(1, D)), None
    return pl.pallas_call(
        _add_norm_kernel, grid=(L // tr,),
        in_specs=[row, row, vec, vec], out_specs=[row, row],
        out_shape=[jax.ShapeDtypeStruct((L, D), _F32), jax.ShapeDtypeStruct((L, D), _BF16)],
        compiler_params=_cparams(("parallel",), 2 * tr * D * 14),
        name="add_norm",
    )(h, mix, g_post.reshape(1, D), g_next.reshape(1, D))


def _mm_kernel(*refs, nk, n_extra, n_out, epilogue):
    a_ref, b_ref = refs[0], refs[1]
    extra = refs[2:2 + n_extra]
    outs = refs[2 + n_extra:2 + n_extra + n_out]
    if nk == 1:
        epilogue(jnp.dot(a_ref[...], b_ref[...], preferred_element_type=_F32), extra, outs)
        return
    acc_ref = refs[-1]
    k = pl.program_id(2)

    @pl.when(k == 0)
    def _():
        acc_ref[...] = jnp.zeros_like(acc_ref)

    acc_ref[...] += jnp.dot(a_ref[...], b_ref[...], preferred_element_type=_F32)

    @pl.when(k == nk - 1)
    def _():
        epilogue(acc_ref[...], extra, outs)


def _matmul(a, b, *, tm, tn, tk, epilogue, out_shapes, out_specs, extras=(), extra_specs=(),
            layer=None, name="matmul"):
    M, K = a.shape
    N = b.shape[-1]
    if layer is None:
        b_spec = pl.BlockSpec((tk, tn), lambda i, j, k: (k, j))
    else:
        b_spec = pl.BlockSpec((None, tk, tn), lambda i, j, k: (layer, k, j))
    nk = K // tk
    kern = functools.partial(_mm_kernel, nk=nk, n_extra=len(extras), n_out=len(out_shapes),
                             epilogue=epilogue)
    out_bytes = sum(math.prod(s.block_shape) * jnp.dtype(o.dtype).itemsize
                    for s, o in zip(out_specs, out_shapes))
    vmem = 2 * (tm * tk * 2 + tk * tn * 2 + out_bytes) + 2 * tm * tn * 4
    vmem += 2 * sum(math.prod(s.block_shape) * 4 for s in extra_specs)
    return pl.pallas_call(
        kern,
        grid=(M // tm, N // tn, nk),
        in_specs=[pl.BlockSpec((tm, tk), lambda i, j, k: (i, k)), b_spec] + list(extra_specs),
        out_specs=list(out_specs),
        out_shape=list(out_shapes),
        scratch_shapes=[pltpu.VMEM((tm, tn), _F32)] if nk > 1 else [],
        compiler_params=_cparams(("parallel", "parallel", "arbitrary"), vmem),
        name=name,
    )(a, b, *extras)


def _ep_store(acc, extra, outs):
    outs[0][...] = acc.astype(outs[0].dtype)


def _ep_relu2(acc, extra, outs):
    r = jnp.maximum(acc, 0.0)
    outs[0][...] = (r * r).astype(outs[0].dtype)


_FULL_K_MAX = 4096


def _mm_tiles(M, K, N):
    tn = _pick(N, (1024, 512, 256, 128))
    if K <= _FULL_K_MAX:
        return _pick(M, (640, 512, 256)), tn, K
    return _pick(M, (1280, 1024, 512, 256)), tn, _pick(K, (2048, 1024, 512, 256, 128))


def _plain_matmul(a, b, out_dtype, *, layer=None, epilogue=_ep_store, name="matmul"):
    M, N = a.shape[0], b.shape[-1]
    tm, tn, tk = _mm_tiles(M, a.shape[1], N)
    return _matmul(a, b, tm=tm, tn=tn, tk=tk, epilogue=epilogue, layer=layer,
                   out_shapes=[jax.ShapeDtypeStruct((M, N), out_dtype)],
                   out_specs=[pl.BlockSpec((tm, tn), lambda i, j, k: (i, j))], name=name)[0]


def _rope128(x, cos, sin):
    lane = lax.broadcasted_iota(jnp.int32, x.shape, 1)
    half = ROPE_DIM // 2
    partner = jnp.where(lane % ROPE_DIM < half,
                        pltpu.roll(x, LANE - half, 1), pltpu.roll(x, half, 1))
    return x * cos + partner * sin


def _ep_mla_a(acc, extra, outs, *, q_rank, kv_rank):
    qn_ref, kvn_ref, cos_ref, sin_ref = extra
    cq_ref, ckv_ref, kpe_ref = outs
    cq_ref[...] = _rms(acc[:, :q_rank], qn_ref[...]).astype(cq_ref.dtype)
    ckv_ref[...] = _rms(acc[:, q_rank:q_rank + kv_rank], kvn_ref[...]).astype(ckv_ref.dtype)
    kpe = acc[:, q_rank + kv_rank:]
    kpe_ref[...] = _rope128(kpe, cos_ref[...], sin_ref[...]).astype(kpe_ref.dtype)


def _ep_mla_q(acc, extra, outs, *, heads, scale):
    cos_ref, sin_ref = extra
    o_ref = outs[0]
    cos = cos_ref[...]
    sin = sin_ref[...]
    for hh in range(heads):
        base = hh * QK_PAD
        o_ref[:, base:base + NOPE_DIM] = (acc[:, base:base + NOPE_DIM] * scale).astype(o_ref.dtype)
        pe = _rope128(acc[:, base + NOPE_DIM:base + QK_PAD], cos, sin)
        o_ref[:, base + NOPE_DIM:base + QK_PAD] = (pe * scale).astype(o_ref.dtype)


def _kv_kernel(c_ref, wk_ref, wv_ref, kpe_ref, k_ref, v_ref, *, heads):
    c = c_ref[...]
    kn = jnp.dot(c, wk_ref[...], preferred_element_type=_F32)
    vv = jnp.dot(c, wv_ref[...], preferred_element_type=_F32)
    kpe = kpe_ref[...]
    ones = jnp.ones((c.shape[0], V_DIM), v_ref.dtype)
    for hh in range(heads):
        k_ref[:, hh * QK_PAD:hh * QK_PAD + NOPE_DIM] = (
            kn[:, hh * NOPE_DIM:(hh + 1) * NOPE_DIM].astype(k_ref.dtype))
        k_ref[:, hh * QK_PAD + NOPE_DIM:(hh + 1) * QK_PAD] = kpe
        v_ref[:, 2 * hh * V_DIM:(2 * hh + 1) * V_DIM] = (
            vv[:, hh * V_DIM:(hh + 1) * V_DIM].astype(v_ref.dtype))
        v_ref[:, (2 * hh + 1) * V_DIM:(2 * hh + 2) * V_DIM] = ones


def _kv_proj(ckv, wk, wv, kpe, tm, heads_per_step):
    L, R = ckv.shape
    H = wk.shape[1] // NOPE_DIM
    g = heads_per_step
    kern = functools.partial(_kv_kernel, heads=g)
    vmem = 2 * (tm * R * 2 + 2 * R * g * 128 * 2 + tm * 128 * 2 + tm * g * 512 * 2) + tm * g * 256 * 8
    return pl.pallas_call(
        kern,
        grid=(L // tm, H // g),
        in_specs=[pl.BlockSpec((tm, R), lambda i, j: (i, 0)),
                  pl.BlockSpec((R, g * NOPE_DIM), lambda i, j: (0, j)),
                  pl.BlockSpec((R, g * V_DIM), lambda i, j: (0, j)),
                  pl.BlockSpec((tm, LANE), lambda i, j: (i, 0))],
        out_specs=[pl.BlockSpec((tm, g * QK_PAD), lambda i, j: (i, j)),
                   pl.BlockSpec((tm, g * 2 * V_DIM), lambda i, j: (i, j))],
        out_shape=[jax.ShapeDtypeStruct((L, H * QK_PAD), _BF16),
                   jax.ShapeDtypeStruct((L, H * 2 * V_DIM), _BF16)],
        compiler_params=_cparams(("parallel", "parallel"), vmem),
        name="mla_kv",
    )(ckv, wk, wv, kpe)


def _attn_kernel(q_ref, k_ref, v_ref, o_ref, s0_sc, s1_sc, p0_sc, p1_sc, m_sc, acc_sc, *, t):
    qi = pl.program_id(1)
    ncol = t // LANE

    def key_rows(j):
        return pl.ds(pl.multiple_of(j * t, t), t)

    def scores(j, s_ref):
        s_ref[...] = lax.dot_general(q_ref[...], k_ref[key_rows(j), :], _NT,
                                     preferred_element_type=_F32)

    def value_product(j, p_ref):
        return jnp.dot(p_ref[...], v_ref[key_rows(j), :], preferred_element_type=_F32)

    def stage(s_ref, p_ref, nxt, s_nxt, prev, p_prev, masked=False):
        scores(nxt, s_nxt)
        cols = [s_ref[:, c * LANE:(c + 1) * LANE] for c in range(ncol)]
        if masked:
            row = lax.broadcasted_iota(jnp.int32, (t, LANE), 0)
            lane = lax.broadcasted_iota(jnp.int32, (t, LANE), 1)
            cols = [jnp.where(lane + c * LANE <= row, col, -jnp.inf)
                    for c, col in enumerate(cols)]
        mx = cols[0]
        for col in cols[1:]:
            mx = jnp.maximum(mx, col)
        rowmax = jnp.max(mx, axis=-1, keepdims=True)
        if prev is None:
            m_new = jnp.broadcast_to(rowmax, m_sc.shape)
        else:
            m_prev = m_sc[...]
            m_new = jnp.maximum(m_prev, rowmax)
            alpha = jnp.exp2(m_prev - m_new)
            acc = acc_sc[...] + value_product(prev, p_prev)
            acc_sc[...] = acc * jnp.concatenate([alpha, alpha], axis=1)
        for c, col in enumerate(cols):
            p_ref[:, c * LANE:(c + 1) * LANE] = jnp.exp2(col - m_new).astype(p_ref.dtype)
        m_sc[...] = m_new

    acc_sc[...] = jnp.zeros(acc_sc.shape, _F32)
    for c0 in range(0, t, MXU_N):
        keys = pl.ds(pl.multiple_of(qi * t + c0, MXU_N), MXU_N)
        s0_sc[c0:, c0:c0 + MXU_N] = lax.dot_general(q_ref[c0:, :], k_ref[keys, :], _NT,
                                                     preferred_element_type=_F32)
        if c0:
            s0_sc[:c0, c0:c0 + MXU_N] = jnp.full((c0, MXU_N), -jnp.inf, _F32)
    stage(s0_sc, p0_sc, 0, s1_sc, None, None, masked=True)

    def quad(i, carry):
        u = 4 * i
        stage(s1_sc, p1_sc, u + 1, s0_sc, jnp.where(i == 0, qi, u - 1), p0_sc)
        stage(s0_sc, p0_sc, u + 2, s1_sc, u, p1_sc)
        stage(s1_sc, p1_sc, u + 3, s0_sc, u + 1, p0_sc)
        stage(s0_sc, p0_sc, jnp.minimum(u + 4, qi - 1), s1_sc, u + 2, p1_sc)
        return carry

    lax.fori_loop(0, qi // 4, quad, 0)
    rest = qi % 4
    u0 = qi - rest

    @pl.when(rest >= 1)
    def _():
        stage(s1_sc, p1_sc, jnp.minimum(u0 + 1, qi - 1), s0_sc, jnp.where(u0 == 0, qi, u0 - 1), p0_sc)

    @pl.when(rest >= 2)
    def _():
        stage(s0_sc, p0_sc, jnp.minimum(u0 + 2, qi - 1), s1_sc, u0, p1_sc)

    @pl.when(rest == 3)
    def _():
        stage(s1_sc, p1_sc, u0 + 2, s0_sc, u0 + 1, p0_sc)

    @pl.when(rest % 2 == 1)
    def _():
        acc_sc[...] = acc_sc[...] + value_product(qi - 1, p1_sc)

    @pl.when(rest % 2 == 0)
    def _():
        acc_sc[...] = acc_sc[...] + value_product(jnp.where(qi == 0, qi, qi - 1), p0_sc)

    acc = acc_sc[...]
    o_ref[...] = (acc[:, :V_DIM] / acc[:, V_DIM:]).astype(o_ref.dtype)


def _attention(q, k, v, t):
    L = q.shape[0]
    H = q.shape[1] // QK_PAD
    kern = functools.partial(_attn_kernel, t=t)
    once = pl.Buffered(1)
    vmem = (L * QK_PAD * 2 + L * 2 * V_DIM * 2 + 2 * (t * QK_PAD * 2 + t * V_DIM * 2)
            + 2 * t * t * 6 + 3 * t * 2 * V_DIM * 4 + t * LANE * 4)
    return pl.pallas_call(
        kern,
        grid=(H, L // t),
        in_specs=[pl.BlockSpec((t, QK_PAD), lambda h, i: (i, h)),
                  pl.BlockSpec((L, QK_PAD), lambda h, i: (0, h), pipeline_mode=once),
                  pl.BlockSpec((L, 2 * V_DIM), lambda h, i: (0, h), pipeline_mode=once)],
        out_specs=pl.BlockSpec((t, V_DIM), lambda h, i: (i, h)),
        out_shape=jax.ShapeDtypeStruct((L, H * V_DIM), _BF16),
        scratch_shapes=[pltpu.VMEM((t, t), _F32), pltpu.VMEM((t, t), _F32),
                        pltpu.VMEM((t, t), _BF16), pltpu.VMEM((t, t), _BF16),
                        pltpu.VMEM((t, LANE), _F32), pltpu.VMEM((t, 2 * V_DIM), _F32)],
        compiler_params=_cparams(("parallel", "arbitrary"), vmem),
        name="mla_attention",
    )(q, k, v)


def _mla(hn, w_a, q_norm, kv_norm, w_qb, w_kb, w_vb, w_o, layer, cos, sin, tiles):
    L, D = hn.shape
    q_rank = q_norm.shape[-1]
    kv_rank = kv_norm.shape[-1]
    heads = w_kb.shape[1] // NOPE_DIM
    tm = _pick(L, (640, 512, 256))
    row_tab = pl.BlockSpec((tm, LANE), lambda i, j, k: (i, 0))
    n_a = w_a.shape[1]
    cq, ckv, kpe = _matmul(
        hn, w_a, tm=tm, tn=n_a, tk=_pick(D, (1024, 512, 256, 128)),
        epilogue=functools.partial(_ep_mla_a, q_rank=q_rank, kv_rank=kv_rank),
        out_shapes=[jax.ShapeDtypeStruct((L, q_rank), _BF16),
                    jax.ShapeDtypeStruct((L, kv_rank), _BF16),
                    jax.ShapeDtypeStruct((L, LANE), _BF16)],
        out_specs=[pl.BlockSpec((tm, q_rank), lambda i, j, k: (i, 0)),
                   pl.BlockSpec((tm, kv_rank), lambda i, j, k: (i, 0)),
                   pl.BlockSpec((tm, LANE), lambda i, j, k: (i, 0))],
        extras=(q_norm.reshape(1, -1), kv_norm.reshape(1, -1), cos, sin),
        extra_specs=(pl.BlockSpec((1, q_rank), lambda i, j, k: (0, 0)),
                     pl.BlockSpec((1, kv_rank), lambda i, j, k: (0, 0)), row_tab, row_tab),
        name="mla_latents")
    g = _pick(heads, (4, 2, 1))
    q = _matmul(
        cq, w_qb, tm=tm, tn=g * QK_PAD, tk=q_rank,
        epilogue=functools.partial(_ep_mla_q, heads=g,
                                   scale=(NOPE_DIM + ROPE_DIM) ** -0.5 * math.log2(math.e)),
        out_shapes=[jax.ShapeDtypeStruct((L, heads * QK_PAD), _BF16)],
        out_specs=[pl.BlockSpec((tm, g * QK_PAD), lambda i, j, k: (i, j))],
        extras=(cos, sin), extra_specs=(row_tab, row_tab), name="mla_q")[0]
    k, v = _kv_proj(ckv, w_kb, w_vb, kpe, tm, g)
    o = _attention(q, k, v, tiles["attn"])
    return _plain_matmul(o, w_o, _F32, layer=layer, name="mla_out")


def _gdn_kernel(pq_ref, pk_ref, pv_ref, pz_ref, wq_ref, wk_ref, wv_ref, ba_ref, gp_ref, on_ref,
                o_ref, xq_sc, xk_sc, xv_sc, s_sc, gt_sc, *, rows, hb, n_heads):
    grp = pl.program_id(0)
    c = pl.program_id(1)
    tail = GDN_CONV - 1
    nch = rows // CHUNK

    @pl.when(c == 0)
    def _():
        zero8 = jnp.zeros((8, hb * GDN_DIM), _F32)
        xq_sc[0:8, :] = zero8
        xk_sc[0:8, :] = zero8
        xv_sc[0:8, :] = zero8
        s_sc[...] = jnp.zeros(s_sc.shape, _F32)

    def conv_silu(p_ref, w_ref, x_sc):
        x_sc[8:8 + rows, :] = p_ref[...].astype(_F32)
        w = w_ref[...]
        y = w[tail:tail + 1, :] * x_sc[8:8 + rows, :]
        for j in range(tail):
            y = y + w[j:j + 1, :] * x_sc[8 - tail + j:8 - tail + j + rows, :]
        x_sc[0:8, :] = x_sc[rows:rows + 8, :]
        return y * _sigmoid(y)

    def l2norm(t):
        return t * lax.rsqrt(jnp.sum(t * t, axis=-1, keepdims=True) + NORM_EPS)

    q_all = conv_silu(pq_ref, wq_ref, xq_sc)
    k_all = conv_silu(pk_ref, wk_ref, xk_sc)
    v_all = conv_silu(pv_ref, wv_ref, xv_sc)

    ba = ba_ref[...]
    lane = lax.broadcasted_iota(jnp.int32, ba.shape, 1)
    xa = ba + gp_ref[1:2, :]
    softplus = jnp.maximum(xa, 0.0) + jnp.log(1.0 + jnp.exp(-jnp.abs(xa)))
    g_all = -jnp.exp(gp_ref[0:1, :]) * softplus
    sig_all = _sigmoid(ba)

    rr = lax.broadcasted_iota(jnp.int32, (rows, rows), 0)
    cc = lax.broadcasted_iota(jnp.int32, (rows, rows), 1)
    lower = (rr // CHUNK == cc // CHUNK) & (rr >= cc)
    strict = rr > cc
    same_sub = rr // SUB == cc // SUB
    eye = (rr == cc).astype(_F32)
    tri = jnp.where(lower, 1.0, 0.0).astype(_BF16)
    onorm = on_ref[...]

    eye_lane = (lax.broadcasted_iota(jnp.int32, (LANE, LANE), 0)
                == lax.broadcasted_iota(jnp.int32, (LANE, LANE), 1)).astype(_BF16)
    gc_all = jnp.zeros((rows, LANE), _F32)
    for part in _split3(g_all):
        gc_all = gc_all + jnp.dot(tri, part, preferred_element_type=_F32)
    gc_t = jnp.zeros((LANE, rows), _F32)
    for part in _split3(gc_all):
        gc_t = gc_t + lax.dot_general(eye_lane, part, _NT, preferred_element_type=_F32)
    gt_sc[...] = gc_t

    hbs = range(hb)
    heads = [grp * hb + hh for hh in hbs]
    hs = [slice(hh * GDN_DIM, (hh + 1) * GDN_DIM) for hh in hbs]
    q = [l2norm(q_all[:, hs[h]]) * (GDN_DIM ** -0.5) for h in hbs]
    k = [l2norm(k_all[:, hs[h]]) for h in hbs]
    v = [v_all[:, hs[h]] for h in hbs]
    beta = [jnp.sum(jnp.where(lane == heads[h], sig_all, 0.0), axis=-1, keepdims=True)
            for h in hbs]
    gc = [jnp.broadcast_to(
        jnp.sum(jnp.where(lane == n_heads + heads[h], gc_all, 0.0), axis=-1, keepdims=True),
        (rows, LANE)) for h in hbs]
    gc_row = [gt_sc[pl.ds(n_heads + heads[h], 1), :] for h in hbs]
    dec = [jnp.exp(jnp.where(lower, jnp.concatenate([gc[h]] * (rows // LANE), axis=1) - gc_row[h],
                             -jnp.inf)) for h in hbs]
    egc = [jnp.exp(gc[h]) for h in hbs]
    g_last = [[gc[h][(ci + 1) * CHUNK - 1:(ci + 1) * CHUNK, :] for ci in range(nch)]
              for h in hbs]
    g_end = [jnp.concatenate([jnp.broadcast_to(gl, (CHUNK, LANE)) for gl in g_last[h]], axis=0)
             for h in hbs]
    kb = [k[h] * beta[h] for h in hbs]
    a_mat = [jnp.where(strict, _bdot(kb[h], k[h], _NT) * dec[h], 0.0) for h in hbs]
    attn = [_bdot(q[h], k[h], _NT) * dec[h] for h in hbs]
    rhs = [jnp.concatenate([kb[h] * egc[h], v[h] * beta[h]], axis=1) for h in hbs]

    p = [jnp.where(same_sub, -a_mat[h], 0.0) for h in hbs]
    t = [eye + p[h] for h in hbs]
    for _ in range(int(math.log2(SUB)) - 1):
        p = [_bdot(p[h], p[h]) for h in hbs]
        t = [t[h] + _bdot(t[h], p[h]) for h in hbs]
    m = [-_bdot(t[h], jnp.where(same_sub, 0.0, a_mat[h])) for h in hbs]
    x = [_bdot(t[h], rhs[h]) for h in hbs]
    levels = int(math.log2(CHUNK // SUB))
    for lvl in range(levels):
        x = [x[h] + _bdot(m[h], x[h]) for h in hbs]
        if lvl + 1 < levels:
            m = [_bdot(m[h], m[h]) for h in hbs]

    ax = [_bdot(attn[h], x[h]) for h in hbs]
    q_eff = [q[h] * egc[h] - ax[h][:, :GDN_DIM] for h in hbs]
    o_intra = [ax[h][:, GDN_DIM:] for h in hbs]
    kd = [k[h] * jnp.exp(g_end[h] - gc[h]) for h in hbs]
    gate = []
    for h in hbs:
        z = pz_ref[:, hs[h]].astype(_F32)
        gate.append(z * _sigmoid(z))

    s = [s_sc[h] for h in hbs]
    for ci in range(nch):
        sl = slice(ci * CHUNK, (ci + 1) * CHUNK)
        kx = [_bdot(kd[h][sl], x[h][sl], _TN) for h in hbs]
        o = [_bdot(q_eff[h][sl], s[h]) + o_intra[h][sl] for h in hbs]
        for h in hbs:
            o_ref[sl, hs[h]] = (_rms(o[h], onorm) * gate[h][sl]).astype(o_ref.dtype)
        s = [s[h] * jnp.exp(g_last[h][ci]) - _bdot(kx[h][:, :GDN_DIM], s[h]) + kx[h][:, GDN_DIM:]
             for h in hbs]
    for h in hbs:
        s_sc[h] = s[h]


def _gdn_mixer(proj, ba, conv_w, gate_params, o_norm, rows, hb):
    L = proj.shape[0]
    H = proj.shape[1] // (4 * GDN_DIM)
    kern = functools.partial(_gdn_kernel, rows=rows, hb=hb, n_heads=H)
    width = hb * GDN_DIM
    groups = H // hb

    def col_block(part):
        return pl.BlockSpec((rows, width), lambda g, c: (c, part * groups + g))

    def w_block(part):
        return pl.BlockSpec((GDN_CONV, width), lambda g, c: (0, part * groups + g))

    itemsize = jnp.dtype(proj.dtype).itemsize
    vmem = (2 * (4 * rows * width * itemsize + rows * LANE * 4 + rows * width * 2)
            + 3 * (rows + 8) * width * 4 + (16 << 20))
    return pl.pallas_call(
        kern,
        grid=(groups, L // rows),
        in_specs=[col_block(0), col_block(1), col_block(2), col_block(3),
                  w_block(0), w_block(1), w_block(2),
                  pl.BlockSpec((rows, LANE), lambda g, c: (c, 0)),
                  pl.BlockSpec((8, LANE), lambda g, c: (0, 0)),
                  pl.BlockSpec((1, GDN_DIM), lambda g, c: (0, 0))],
        out_specs=pl.BlockSpec((rows, width), lambda g, c: (c, g)),
        out_shape=jax.ShapeDtypeStruct((L, H * GDN_DIM), _BF16),
        scratch_shapes=[pltpu.VMEM((rows + 8, width), _F32)] * 3
                       + [pltpu.VMEM((hb, GDN_DIM, GDN_DIM), _F32), pltpu.VMEM((LANE, rows), _F32)],
        compiler_params=_cparams(("parallel", "arbitrary"), vmem),
        name="gdn_delta_rule",
    )(proj, proj, proj, proj, conv_w, conv_w, conv_w, ba, gate_params, o_norm.reshape(1, -1))


def _gdn(hn, w_qkvz, w_ba, conv_w, gate_params, o_norm, w_o, layer, tiles):
    heads = w_qkvz.shape[-1] // (4 * GDN_DIM)
    proj = _plain_matmul(hn, w_qkvz, _F32, layer=layer, name="gdn_qkvz")
    ba = _plain_matmul(hn, w_ba, _F32, name="gdn_ba")
    o = _gdn_mixer(proj, ba, conv_w, gate_params, o_norm, tiles["gdn_rows"],
                   _pick(heads, (8, 4, 2, 1)))
    return _plain_matmul(o, w_o, _F32, layer=layer, name="gdn_out")


def _mlp(hn, w_up, w_down, layer, tiles):
    up = _plain_matmul(hn, w_up, _BF16, layer=layer, epilogue=_ep_relu2, name="mlp_up")
    return _plain_matmul(up, w_down, _F32, layer=layer, name="mlp_down")


def _rope_tables(length):
    half = ROPE_DIM // 2
    inv = ROPE_THETA ** (-jnp.arange(0, ROPE_DIM, 2, dtype=_F32) / ROPE_DIM)
    ang = jnp.arange(length, dtype=_F32)[:, None] * inv[None, :]
    cos, sin = jnp.cos(ang), jnp.sin(ang)
    zeros = jnp.zeros((length, LANE - ROPE_DIM), _F32)
    return (jnp.concatenate([cos, cos, zeros], axis=1),
            jnp.concatenate([-sin, sin, zeros], axis=1))


def _mla_weights(wq_a, wq_b, wkv_a, wkv_b, kv_rank):
    D = wq_a.shape[0]
    heads = wq_b.shape[1] // (NOPE_DIM + ROPE_DIM)
    w_a = jnp.concatenate(
        [wq_a, wkv_a, jnp.zeros((D, LANE - ROPE_DIM), wq_a.dtype)], axis=1).astype(_BF16)
    qb = wq_b.reshape(-1, heads, NOPE_DIM + ROPE_DIM)
    qb = jnp.pad(qb, ((0, 0), (0, 0), (0, QK_PAD - NOPE_DIM - ROPE_DIM)))
    w_qb = qb.reshape(-1, heads * QK_PAD).astype(_BF16)
    kvb = wkv_b.reshape(kv_rank, heads, NOPE_DIM + V_DIM)
    w_kb = kvb[:, :, :NOPE_DIM].reshape(kv_rank, heads * NOPE_DIM).astype(_BF16)
    w_vb = kvb[:, :, NOPE_DIM:].reshape(kv_rank, heads * V_DIM).astype(_BF16)
    return w_a, w_qb, w_kb, w_vb


def kernel(x, meta_tokens, norm_gains, mla_wq_a, mla_q_norm, mla_wq_b, mla_wkv_a, mla_kv_norm,
           mla_wkv_b, mla_wo, gdn_w_qkvz, gdn_w_ba, gdn_conv_w, gdn_a_log, gdn_dt_bias,
           gdn_o_norm, gdn_wo, mlp_w_up, mlp_w_down):
    batch, seq, D = x.shape
    n_meta = meta_tokens.shape[0]
    depth = norm_gains.shape[0]
    L = n_meta + seq
    Lp = -(-L // _ROW_ALIGN) * _ROW_ALIGN
    gdn_heads = gdn_w_ba.shape[-1] // 2
    assert 2 * gdn_heads <= LANE and gdn_w_qkvz.shape[-1] == 4 * gdn_heads * GDN_DIM
    tiles = {
        "tr": 256,
        "attn": _pick(Lp, (1280, 1024, 768, 512, 256)),
        "gdn_rows": 256,
    }
    cos, sin = _rope_tables(Lp)
    mla_wo_b, gdn_qkvz_b, gdn_wo_b, up_b, down_b = (
        w.astype(_BF16) for w in (mla_wo, gdn_w_qkvz, gdn_wo, mlp_w_up, mlp_w_down))
    outs = []
    for b in range(batch):
        h = jnp.concatenate([meta_tokens.astype(x.dtype), x[b],
                             jnp.zeros((Lp - L, D), x.dtype)], axis=0)
        hn = _prenorm(h, norm_gains[0, 0], tiles["tr"])
        for i in range(depth):
            gains = norm_gains[i]
            j = i // 2
            if i % 2 == 0:
                kv_rank = mla_kv_norm.shape[-1]
                w = _mla_weights(mla_wq_a[j], mla_wq_b[j], mla_wkv_a[j], mla_wkv_b[j], kv_rank)
                mix = _mla(hn, w[0], mla_q_norm[j], mla_kv_norm[j], w[1], w[2], w[3], mla_wo_b, j,
                           cos, sin, tiles)
            else:
                w_ba = jnp.pad(gdn_w_ba[j], ((0, 0), (0, LANE - 2 * gdn_heads))).astype(_BF16)
                gate_params = jnp.zeros((8, LANE), _F32)
                gate_params = gate_params.at[0, gdn_heads:2 * gdn_heads].set(gdn_a_log[j])
                gate_params = gate_params.at[1, gdn_heads:2 * gdn_heads].set(gdn_dt_bias[j])
                mix = _gdn(hn, gdn_qkvz_b, w_ba, gdn_conv_w[j], gate_params, gdn_o_norm[j],
                           gdn_wo_b, j, tiles)
            h, hn = _add_norm(h, mix, gains[1], gains[2], tiles["tr"])
            ff = _mlp(hn, up_b, down_b, i, tiles)
            g_next = norm_gains[i + 1, 0] if i + 1 < depth else None
            h, hn = _add_norm(h, ff, gains[3], g_next, tiles["tr"])
        outs.append(h[n_meta:L])
    return jnp.stack(outs, axis=0)
```
